```python
import math
import jax
import jax.numpy as jnp
from jax import lax
import numpy as np

D_MODEL = 1024
BATCH = 8
SEQ = 4096
DEPTH = 4

GRID_W = 64
CTX_LEN = 256
N_MIXERS = 2
NORM_EPS = 1e-6
CHUNK = 128

SSD_EXPAND = 2
SSD_D_INNER = SSD_EXPAND * D_MODEL
SSD_HEAD_DIM = 64
SSD_HEADS = SSD_D_INNER // SSD_HEAD_DIM
SSD_GROUPS = 4
SSD_STATE = 128
SSD_CONV = 5
SSD_CONV_DIM = SSD_D_INNER + 2 * SSD_GROUPS * SSD_STATE
SSD_IN_DIM = SSD_D_INNER + SSD_CONV_DIM + 2 * SSD_HEADS
DT_MIN = 1e-3
DT_MAX = 1e-1

RET_HEADS = 4
RET_QK_DIM = D_MODEL // RET_HEADS
RET_V_DIM = 2 * RET_QK_DIM
RET_QK_TOTAL = RET_HEADS * RET_QK_DIM
RET_V_TOTAL = RET_HEADS * RET_V_DIM
RET_IN_DIM = 2 * RET_QK_TOTAL + 2 * RET_V_TOTAL
ROPE_BASE = 10000.0

N_EXPERTS = 32
TOP_K = 4
D_FF = D_MODEL
SWIGLU_ALPHA = 1.702
SWIGLU_LIMIT = 7.0
MOE_BLOCK = 128

N_SSD_LAYERS = (DEPTH + 1) // 2
N_RET_LAYERS = DEPTH // 2

kernel_name = 'hybrid_ssd_retention_moe_dit'


def rms_normalise(x):
    xf = x.astype(jnp.float32)
    return (xf * lax.rsqrt(jnp.mean(xf * xf, axis=-1, keepdims=True) + NORM_EPS)).astype(x.dtype)


def rms_norm(x, g):
    return rms_normalise(x) * g


def modulate(h, shift, scale):
    return h * (1 + scale[:, None, :]) + shift[:, None, :]


def depthwise_conv_centred(x, w, b):
    k_w, ch = w.shape
    y = lax.conv_general_dilated(x, w[:, None, :].astype(x.dtype), window_strides=(1,),
                                 padding=[(k_w // 2, k_w // 2)],
                                 dimension_numbers=('NWC', 'WIO', 'NWC'),
                                 feature_group_count=ch)
    return y + b


def rope_half(u, pos):
    d = u.shape[-1]
    inv_freq = ROPE_BASE ** (-jnp.arange(0, d, 2, dtype=jnp.float32) / d)
    ang = pos[:, None] * inv_freq[None, :]
    cos = jnp.cos(ang)[None, :, None, :]
    sin = jnp.sin(ang)[None, :, None, :]
    u1, u2 = jnp.split(u.astype(jnp.float32), 2, axis=-1)
    return jnp.concatenate([u1 * cos - u2 * sin, u1 * sin + u2 * cos], axis=-1).astype(u.dtype)


def axial_rope(u, row_pos, col_pos):
    a, b = jnp.split(u, 2, axis=-1)
    return jnp.concatenate([rope_half(a, row_pos), rope_half(b, col_pos)], axis=-1)


def chunked_scan(q, k, v, log_a, h0, with_output=True):
    bsz, seq_len, n_groups, d_state = q.shape
    n_heads, d_v = v.shape[2], v.shape[3]
    rep = n_heads // n_groups
    nc = seq_len // CHUNK
    qc = q.reshape(bsz, nc, CHUNK, n_groups, d_state)
    kc = k.reshape(bsz, nc, CHUNK, n_groups, d_state)
    vc = v.reshape(bsz, nc, CHUNK, n_groups, rep, d_v)
    acum = jnp.cumsum(log_a.astype(jnp.float32).reshape(bsz, nc, CHUNK, n_groups, rep), axis=2)
    acum_t = jnp.moveaxis(acum, 2, -1)
    decay_to_end = jnp.exp(acum_t[..., -1:] - acum_t)
    states = jnp.einsum('bcjgn,bcgrj,bcjgrp->bcgrpn', kc, decay_to_end, vc)
    chunk_decay = jnp.exp(acum_t[..., -1])

    def step(h, inp):
        st, dec = inp
        return h * dec[..., None, None] + st, h

    h_init = h0.reshape(bsz, n_groups, rep, d_v, d_state).astype(states.dtype)
    h_final, h_enter = lax.scan(step, h_init,
                                (jnp.moveaxis(states, 1, 0), jnp.moveaxis(chunk_decay, 1, 0)))
    h_final = h_final.reshape(bsz, n_heads, d_v, d_state)
    if not with_output:
        return None, h_final
    h_enter = jnp.moveaxis(h_enter, 0, 1)
    causal = jnp.tril(jnp.ones((CHUNK, CHUNK), dtype=bool))
    seg = acum_t[..., :, None] - acum_t[..., None, :]
    decay = jnp.exp(jnp.where(causal, seg, -jnp.inf))
    scores = jnp.einsum('bcign,bcjgn->bcgij', qc, kc)
    y_intra = jnp.einsum('bcgrij,bcjgrp->bcigrp', scores[:, :, :, None] * decay, vc)
    y_inter = jnp.einsum('bcign,bcgrpn->bcigrp', qc, h_enter) * jnp.exp(acum)[..., None]
    return (y_intra + y_inter).reshape(bsz, seq_len, n_heads, d_v), h_final


def flip_seq(*arrays):
    return tuple(jnp.flip(a, axis=1) for a in arrays)


def prefix_bidirectional_scan(ctx_f, ctx_b, lat_f, lat_b, need_ctx_out):
    k_l, v_l = lat_f[1], lat_f[2]
    h0 = jnp.zeros((v_l.shape[0], v_l.shape[2], v_l.shape[3], k_l.shape[3]), jnp.float32)
    y_cf, h_cf = chunked_scan(*ctx_f, h0, with_output=need_ctx_out)
    y_cb, h_cb = chunked_scan(*flip_seq(*ctx_b), h0, with_output=need_ctx_out)
    y_lf, _ = chunked_scan(*lat_f, h_cf)
    y_lb, _ = chunked_scan(*flip_seq(*lat_b), h_cb)
    y_lat = y_lf + jnp.flip(y_lb, axis=1)
    y_ctx = y_cf + jnp.flip(y_cb, axis=1) if need_ctx_out else None
    return y_ctx, y_lat


def ssd_mixer(h_ctx, h_lat, w_in, conv_w, conv_b, dt_bias_f, dt_bias_b, a_log_f, a_log_b,
              d_skip, norm_g, w_out, need_ctx_out):
    a_f = -jnp.exp(a_log_f.astype(jnp.float32))
    a_b = -jnp.exp(a_log_b.astype(jnp.float32))

    def project(h):
        bsz, seq_len, _ = h.shape
        z, xbc, dt_f, dt_b = jnp.split(h @ w_in, [SSD_D_INNER, SSD_D_INNER + SSD_CONV_DIM,
                                                  SSD_D_INNER + SSD_CONV_DIM + SSD_HEADS], axis=-1)
        xbc = jax.nn.silu(depthwise_conv_centred(xbc, conv_w, conv_b))
        xs, b_in, c_out = jnp.split(xbc, [SSD_D_INNER, SSD_D_INNER + SSD_GROUPS * SSD_STATE], axis=-1)
        xs = xs.reshape(bsz, seq_len, SSD_HEADS, SSD_HEAD_DIM)
        b_in = b_in.reshape(bsz, seq_len, SSD_GROUPS, SSD_STATE)
        c_out = c_out.reshape(bsz, seq_len, SSD_GROUPS, SSD_STATE)
        dt_f = jax.nn.softplus(dt_f.astype(jnp.float32) + dt_bias_f)
        dt_b = jax.nn.softplus(dt_b.astype(jnp.float32) + dt_bias_b)
        fwd = (c_out, b_in, xs * dt_f[..., None], dt_f * a_f)
        bwd = (c_out, b_in, xs * dt_b[..., None], dt_b * a_b)
        return z, xs, fwd, bwd

    def readout(y, z, xs):
        bsz, seq_len = y.shape[:2]
        y = (y + xs * d_skip[:, None]).reshape(bsz, seq_len, SSD_D_INNER) * jax.nn.silu(z)
        y = rms_normalise(y.reshape(bsz, seq_len, SSD_GROUPS, SSD_D_INNER // SSD_GROUPS))
        y = y.reshape(bsz, seq_len, SSD_D_INNER) * norm_g
        return (y @ w_out).astype(z.dtype)

    z_l, xs_l, lat_f, lat_b = project(h_lat)
    z_c, xs_c, ctx_f, ctx_b = project(h_ctx)
    y_ctx, y_lat = prefix_bidirectional_scan(ctx_f, ctx_b, lat_f, lat_b, need_ctx_out)
    out_ctx = readout(y_ctx, z_c, xs_c) if need_ctx_out else None
    return out_ctx, readout(y_lat, z_l, xs_l)


def retention_mixer(h_ctx, h_lat, row_pos, col_pos, w_in, scale_f, scale_b, w_out, need_ctx_out):
    log_gamma_f = jnp.log1p(-jnp.exp2(-scale_f.astype(jnp.float32)))
    log_gamma_b = jnp.log1p(-jnp.exp2(-scale_b.astype(jnp.float32)))

    def project(h, rotate):
        bsz, seq_len, _ = h.shape
        q, k, v, g = jnp.split(h @ w_in, [RET_QK_TOTAL, 2 * RET_QK_TOTAL,
                                          2 * RET_QK_TOTAL + RET_V_TOTAL], axis=-1)
        q = q.reshape(bsz, seq_len, RET_HEADS, RET_QK_DIM)
        k = k.reshape(bsz, seq_len, RET_HEADS, RET_QK_DIM)
        if rotate:
            q = axial_rope(q, row_pos, col_pos)
            k = axial_rope(k, row_pos, col_pos)
        k = k * RET_QK_DIM ** -0.5
        v = v.reshape(bsz, seq_len, RET_HEADS, RET_V_DIM)
        la_f = jnp.broadcast_to(log_gamma_f, (bsz, seq_len, RET_HEADS))
        la_b = jnp.broadcast_to(log_gamma_b, (bsz, seq_len, RET_HEADS))
        return g, (q, k, v, la_f), (q, k, v, la_b)

    def readout(y, g):
        bsz, seq_len = y.shape[:2]
        y = rms_normalise(y).reshape(bsz, seq_len, RET_V_TOTAL)
        return ((jax.nn.silu(g) * y) @ w_out).astype(g.dtype)

    g_l, lat_f, lat_b = project(h_lat, True)
    g_c, ctx_f, ctx_b = project(h_ctx, False)
    y_ctx, y_lat = prefix_bidirectional_scan(ctx_f, ctx_b, lat_f, lat_b, need_ctx_out)
    out_ctx = readout(y_ctx, g_c) if need_ctx_out else None
    return out_ctx, readout(y_lat, g_l)


def moe_ffn(x, w_router, b_router, w1, b1, w2, b2):
    bsz, seq_len, d = x.shape
    xf = x.reshape(-1, d)
    n_tok = xf.shape[0]
    logits = (xf @ w_router + b_router).astype(jnp.float32)
    top_v, top_i = lax.top_k(logits, TOP_K)
    gates = jax.nn.softmax(top_v, axis=-1)
    n_assign = n_tok * TOP_K
    e_flat = top_i.reshape(-1)
    tok_flat = jnp.repeat(jnp.arange(n_tok, dtype=jnp.int32), TOP_K)
    order = jnp.argsort(e_flat)
    e_sorted, tok_sorted, g_sorted = e_flat[order], tok_flat[order], gates.reshape(-1)[order]
    counts = jnp.bincount(e_flat, length=N_EXPERTS)
    padded = (counts + MOE_BLOCK - 1) // MOE_BLOCK * MOE_BLOCK
    start = jnp.cumsum(counts) - counts
    pend = jnp.cumsum(padded)
    pstart = pend - padded
    dest = pstart[e_sorted] + (jnp.arange(n_assign) - start[e_sorted])
    n_blocks = -(-n_assign // MOE_BLOCK) + N_EXPERTS
    cap = n_blocks * MOE_BLOCK
    row_tok = jnp.full((cap,), n_tok, jnp.int32).at[dest].set(tok_sorted)
    row_gate = jnp.zeros((cap,), gates.dtype).at[dest].set(g_sorted)
    block_expert = jnp.minimum(
        jnp.searchsorted(pend, jnp.arange(n_blocks) * MOE_BLOCK, side='right'), N_EXPERTS - 1)
    x_pad = jnp.concatenate([xf, jnp.zeros((1, d), xf.dtype)], axis=0)
    x_blocks = x_pad[row_tok].reshape(n_blocks, MOE_BLOCK, d)

    def expert_block(args):
        xb, e = args
        hdn = xb @ w1[e] + b1[e]
        x_glu = jnp.minimum(hdn[..., ::2], SWIGLU_LIMIT)
        x_lin = jnp.clip(hdn[..., 1::2], -SWIGLU_LIMIT, SWIGLU_LIMIT)
        act = x_glu * jax.nn.sigmoid(SWIGLU_ALPHA * x_glu) * (x_lin + 1)
        return act @ w2[e] + b2[e]

    y_blocks = lax.map(expert_block, (x_blocks, block_expert))
    y_rows = y_blocks.reshape(cap, d) * row_gate[:, None]
    out = jnp.zeros((n_tok + 1, d), y_rows.dtype).at[row_tok].add(y_rows)[:n_tok]
    return out.reshape(bsz, seq_len, d).astype(x.dtype)


def setup_inputs(seed: int = 0) -> dict:
    key = jax.random.key(seed)
    ks = jax.random.split(key, 32)
    f32 = jnp.float32

    def nrm(k, shape, std):
        return std * jax.random.normal(k, shape, f32)

    n_ssd, n_ret = N_SSD_LAYERS, N_RET_LAYERS
    dt_f0 = jnp.exp(jax.random.uniform(ks[10], (n_ssd, SSD_HEADS), f32, math.log(DT_MIN), math.log(DT_MAX)))
    dt_b0 = jnp.exp(jax.random.uniform(ks[11], (n_ssd, SSD_HEADS), f32, math.log(DT_MIN), math.log(DT_MAX)))
    ret_base = 5.0 + jnp.arange(RET_HEADS, dtype=f32)[None, :]
    return {
        'x': nrm(ks[0], (BATCH, SEQ, D_MODEL), 1.0),
        'c': nrm(ks[1], (BATCH, D_MODEL), 1.0),
        'ctx': nrm(ks[2], (BATCH, CTX_LEN, D_MODEL), 1.0),
        'c_ctx': nrm(ks[3], (D_MODEL,), 1.0),
        'w_mod': nrm(ks[4], (DEPTH, D_MODEL, 6 * D_MODEL), 0.5 * D_MODEL ** -0.5),
        'b_mod': nrm(ks[5], (DEPTH, 6 * D_MODEL), 0.01),
        'norm1_g': 1.0 + nrm(ks[6], (DEPTH, D_MODEL), 0.05),
        'norm2_g': 1.0 + nrm(ks[7], (DEPTH, D_MODEL), 0.05),
        'ssd_w_in': nrm(ks[8], (n_ssd, D_MODEL, SSD_IN_DIM), D_MODEL ** -0.5),
        'ssd_conv_w': nrm(ks[9], (n_ssd, SSD_CONV, SSD_CONV_DIM), SSD_CONV ** -0.5),
        'ssd_conv_b': nrm(ks[12], (n_ssd, SSD_CONV_DIM), 0.01),
        'ssd_dt_bias_f': dt_f0 + jnp.log(-jnp.expm1(-dt_f0)),
        'ssd_dt_bias_b': dt_b0 + jnp.log(-jnp.expm1(-dt_b0)),
        'ssd_a_log_f': jnp.log(jax.random.uniform(ks[13], (n_ssd, SSD_HEADS), f32, 1.0, 16.0)),
        'ssd_a_log_b': jnp.log(jax.random.uniform(ks[14], (n_ssd, SSD_HEADS), f32, 1.0, 16.0)),
        'ssd_d': 1.0 + nrm(ks[15], (n_ssd, SSD_HEADS), 0.1),
        'ssd_norm_g': 1.0 + nrm(ks[16], (n_ssd, SSD_D_INNER), 0.05),
        'ssd_w_out': nrm(ks[17], (n_ssd, SSD_D_INNER, D_MODEL), SSD_D_INNER ** -0.5),
        'ret_w_in': nrm(ks[18], (n_ret, D_MODEL, RET_IN_DIM), D_MODEL ** -0.5),
        'ret_scale_f': ret_base + nrm(ks[19], (n_ret, RET_HEADS), 0.1),
        'ret_scale_b': ret_base + nrm(ks[20], (n_ret, RET_HEADS), 0.1),
        'ret_w_out': nrm(ks[21], (n_ret, RET_V_TOTAL, D_MODEL), RET_V_TOTAL ** -0.5),
        'router_w': nrm(ks[22], (DEPTH, D_MODEL, N_EXPERTS), D_MODEL ** -0.5),
        'router_b': nrm(ks[23], (DEPTH, N_EXPERTS), 0.01),
        'moe_w1': nrm(ks[24], (DEPTH, N_EXPERTS, D_MODEL, 2 * D_FF), D_MODEL ** -0.5),
        'moe_b1': nrm(ks[25], (DEPTH, N_EXPERTS, 2 * D_FF), 0.01),
        'moe_w2': nrm(ks[26], (DEPTH, N_EXPERTS, D_FF, D_MODEL), D_FF ** -0.5),
        'moe_b2': nrm(ks[27], (DEPTH, N_EXPERTS, D_MODEL), 0.01),
        'final_g': 1.0 + nrm(ks[28], (D_MODEL,), 0.05),
    }


def reference(x, c, ctx, c_ctx, w_mod, b_mod, norm1_g, norm2_g, ssd_w_in, ssd_conv_w, ssd_conv_b,
              ssd_dt_bias_f, ssd_dt_bias_b, ssd_a_log_f, ssd_a_log_b, ssd_d, ssd_norm_g, ssd_w_out,
              ret_w_in, ret_scale_f, ret_scale_b, ret_w_out, router_w, router_b,
              moe_w1, moe_b1, moe_w2, moe_b2, final_g):
    seq_len = x.shape[1]
    ctx_len = ctx.shape[1]
    rows = seq_len // GRID_W
    t = jnp.arange(seq_len)
    row_pos = (t // GRID_W).astype(jnp.float32) - (rows - 1) / 2.0
    col_pos = (t % GRID_W).astype(jnp.float32) - (GRID_W - 1) / 2.0
    cond_lat = jax.nn.silu(c)
    cond_ctx = jax.nn.silu(c_ctx)[None, :]
    x_lat, x_ctx = x, ctx
    for i in range(DEPTH):
        need_ctx = i < DEPTH - 1
        j = i // N_MIXERS
        sh1_l, sc1_l, g1_l, sh2_l, sc2_l, g2_l = jnp.split(cond_lat @ w_mod[i] + b_mod[i], 6, axis=-1)
        sh1_c, sc1_c, g1_c, sh2_c, sc2_c, g2_c = jnp.split(cond_ctx @ w_mod[i] + b_mod[i], 6, axis=-1)
        h_lat = modulate(rms_norm(x_lat, norm1_g[i]), sh1_l, sc1_l)
        h_ctx = modulate(rms_norm(x_ctx, norm1_g[i]), sh1_c, sc1_c)
        if i % N_MIXERS == 0:
            m_ctx, m_lat = ssd_mixer(h_ctx, h_lat, ssd_w_in[j], ssd_conv_w[j], ssd_conv_b[j],
                                     ssd_dt_bias_f[j], ssd_dt_bias_b[j], ssd_a_log_f[j], ssd_a_log_b[j],
                                     ssd_d[j], ssd_norm_g[j], ssd_w_out[j], need_ctx)
        else:
            m_ctx, m_lat = retention_mixer(h_ctx, h_lat, row_pos, col_pos, ret_w_in[j],
                                           ret_scale_f[j], ret_scale_b[j], ret_w_out[j], need_ctx)
        x_lat = x_lat + g1_l[:, None, :] * m_lat
        h_lat = modulate(rms_norm(x_lat, norm2_g[i]), sh2_l, sc2_l)
        if need_ctx:
            x_ctx = x_ctx + g1_c[:, None, :] * m_ctx
            h_ctx = modulate(rms_norm(x_ctx, norm2_g[i]), sh2_c, sc2_c)
            f_all = moe_ffn(jnp.concatenate([h_ctx, h_lat], axis=1), router_w[i], router_b[i],
                            moe_w1[i], moe_b1[i], moe_w2[i], moe_b2[i])
            x_ctx = x_ctx + g2_c[:, None, :] * f_all[:, :ctx_len]
            f_lat = f_all[:, ctx_len:]
        else:
            f_lat = moe_ffn(h_lat, router_w[i], router_b[i], moe_w1[i], moe_b1[i], moe_w2[i], moe_b2[i])
        x_lat = x_lat + g2_l[:, None, :] * f_lat
    return rms_norm(x_lat, final_g)
```

```python
import functools
import math

import jax
import jax.numpy as jnp
from jax import lax
from jax.experimental import pallas as pl
from jax.experimental.pallas import tpu as pltpu

F32 = jnp.float32
BF16 = jnp.bfloat16

NORM_EPS = 1e-6
CHUNK = 128
GRID_W = 64
ROPE_BASE = 10000.0

SSD_HEAD_DIM = 64
SSD_GROUPS = 4
SSD_STATE = 128
SSD_CONV = 5

RET_HEADS = 4

TOP_K = 4
SWIGLU_ALPHA = 1.702
SWIGLU_LIMIT = 7.0

LANES = 128
ROW_TILE = 256
EXPERT_ROWS = 256
HALO = 16
VMEM_LIMIT = 56 * 1024 * 1024
NEG_BIG = -1e30


def _cparams(sem):
    return pltpu.CompilerParams(dimension_semantics=sem, vmem_limit_bytes=VMEM_LIMIT)


def _split_bf16(a):
    hi = a.astype(BF16)
    lo = (a - hi.astype(F32)).astype(BF16)
    return hi, lo


def _dot(a, b):
    return jnp.dot(a, b, preferred_element_type=F32)


def _dot_tn(a, b):
    return lax.dot_general(a, b, (((0,), (0,)), ((), ())), preferred_element_type=F32)


def _dot_nt(a, b):
    return lax.dot_general(a, b, (((1,), (1,)), ((), ())), preferred_element_type=F32)


def _dot_split_lhs(a, b_bf16):
    hi, lo = _split_bf16(a)
    return _dot(hi, b_bf16) + _dot(lo, b_bf16)


def _sigmoid(x):
    return 1.0 / (1.0 + jnp.exp(-x))


def _silu(x):
    return x * _sigmoid(x)


def _mod_kernel(cond_ref, w_ref, b_ref, out_ref):
    s = _silu(cond_ref[...])
    s_hi, s_lo = _split_bf16(s)
    w_hi, w_lo = _split_bf16(w_ref[0])
    acc = _dot(s_hi, w_hi) + _dot(s_lo, w_hi) + _dot(s_hi, w_lo)
    out_ref[0] = acc + b_ref[0]


def _modulation(cond, w_mod, b_mod):
    depth, d, n = w_mod.shape
    rows = cond.shape[0]
    tn = 1024
    return pl.pallas_call(
        _mod_kernel,
        out_shape=jax.ShapeDtypeStruct((depth, rows, n), F32),
        grid=(depth, n // tn),
        in_specs=[
            pl.BlockSpec((rows, d), lambda l, j: (0, 0)),
            pl.BlockSpec((1, d, tn), lambda l, j: (l, 0, j)),
            pl.BlockSpec((1, 1, tn), lambda l, j: (l, 0, j)),
        ],
        out_specs=pl.BlockSpec((1, rows, tn), lambda l, j: (l, 0, j)),
        compiler_params=_cparams(("arbitrary", "arbitrary")),
    )(cond, w_mod, b_mod.reshape(depth, 1, n))


def _norm_modulate(x, g, mod, shift_idx, scale_idx):
    ms = jnp.mean(x * x, axis=-1, keepdims=True)
    h = x * lax.rsqrt(ms + NORM_EPS) * g
    return h * (1.0 + mod[scale_idx:scale_idx + 1]) + mod[shift_idx:shift_idx + 1]


def _in_proj_kernel(*refs, splits, rope_outs, k_scale):
    if rope_outs:
        x_ref, g_ref, mod_ref, w_ref, cos_ref, sin_ref = refs[:6]
        out_refs = refs[6:]
    else:
        x_ref, g_ref, mod_ref, w_ref = refs[:4]
        out_refs = refs[4:]
    h = _norm_modulate(x_ref[0], g_ref[...], mod_ref[0], 0, 1).astype(BF16)
    for idx, (start, width) in enumerate(splits):
        o = _dot(h, w_ref[:, start:start + width])
        if idx in rope_outs:
            cos = cos_ref[...]
            sin = sin_ref[...]
            pieces = []
            for j in range(width // LANES):
                u = o[:, j * LANES:(j + 1) * LANES]
                half = (j % 2) * LANES
                r = (u * cos[:, half:half + LANES]
                     + pltpu.roll(u, LANES // 2, axis=1) * sin[:, half:half + LANES])
                pieces.append(r)
            o = jnp.concatenate(pieces, axis=1)
            if idx == rope_outs[-1]:
                o = o * k_scale
        out_refs[idx][0] = o.astype(out_refs[idx].dtype)


def _in_proj(x, g, mod, w, splits, out_dtypes, ctx_tiles, rope=None, k_scale=1.0):
    n_batch, s_len, d = x.shape
    n_tiles = s_len // ROW_TILE
    in_specs = [
        pl.BlockSpec((1, ROW_TILE, d), lambda b, t: (b, t, 0)),
        pl.BlockSpec((1, d), lambda b, t: (0, 0)),
        pl.BlockSpec((1, 6, d), lambda b, t: (jnp.where(t < ctx_tiles, n_batch, b), 0, 0)),
        pl.BlockSpec(w.shape, lambda b, t: (0, 0)),
    ]
    args = [x, g.reshape(1, d), mod, w]
    rope_outs = ()
    if rope is not None:
        cos, sin, rope_outs = rope
        in_specs += [pl.BlockSpec((ROW_TILE, cos.shape[1]), lambda b, t: (t, 0)),
                     pl.BlockSpec((ROW_TILE, sin.shape[1]), lambda b, t: (t, 0))]
        args += [cos, sin]
    out_shape = [jax.ShapeDtypeStruct((n_batch, s_len, wd), dt)
                 for (_, wd), dt in zip(splits, out_dtypes)]
    out_specs = [pl.BlockSpec((1, ROW_TILE, wd), lambda b, t: (b, t, 0)) for _, wd in splits]
    return pl.pallas_call(
        functools.partial(_in_proj_kernel, splits=tuple(splits), rope_outs=tuple(rope_outs),
                          k_scale=k_scale),
        out_shape=out_shape,
        grid=(n_batch, n_tiles),
        in_specs=in_specs,
        out_specs=out_specs,
        compiler_params=_cparams(("parallel", "arbitrary")),
    )(*args)


def _ssd_conv_kernel(cur_ref, prev_ref, next_ref, dt_ref, cw_ref, cb_ref, dtb_ref, alog_ref,
                     xbc_ref, dtla_ref, ext_ref, *, ctx_chunks, n_chunks):
    c = pl.program_id(1)
    has_prev = jnp.logical_and(c != 0, c != ctx_chunks)
    has_next = jnp.logical_and(c != ctx_chunks - 1, c != n_chunks - 1)
    pad = SSD_CONV // 2
    ext_ref[0:8, :] = prev_ref[0].astype(F32)[HALO - 8:HALO, :] * jnp.where(has_prev, 1.0, 0.0)
    ext_ref[8:8 + CHUNK, :] = cur_ref[0].astype(F32)
    ext_ref[8 + CHUNK:16 + CHUNK, :] = next_ref[0].astype(F32)[0:8, :] * jnp.where(has_next, 1.0, 0.0)
    cw = cw_ref[...]
    acc = jnp.zeros((CHUNK, cur_ref.shape[2]), F32) + cb_ref[...]
    for k in range(SSD_CONV):
        acc = acc + ext_ref[8 - pad + k:8 - pad + k + CHUNK, :] * cw[k:k + 1, :]
    xbc_ref[0] = _silu(acc).astype(xbc_ref.dtype)
    raw = dt_ref[0] + dtb_ref[...]
    dt = jnp.maximum(raw, 0.0) + jnp.log(1.0 + jnp.exp(-jnp.abs(raw)))
    lane = lax.broadcasted_iota(jnp.int32, dt.shape, 1)
    a = -jnp.exp(alog_ref[...])
    dtla_ref[0] = jnp.where(lane < LANES // 2, dt, dt * a)


def _ssd_conv(xbc, dt_raw, conv_w, conv_b, dt_bias4, a_log4, ctx_chunks):
    n_batch, s_len, cdim = xbc.shape
    n_chunks = s_len // CHUNK
    per = CHUNK // HALO
    last_halo = s_len // HALO - 1
    return pl.pallas_call(
        functools.partial(_ssd_conv_kernel, ctx_chunks=ctx_chunks, n_chunks=n_chunks),
        out_shape=[jax.ShapeDtypeStruct((n_batch, s_len, cdim), BF16),
                   jax.ShapeDtypeStruct((n_batch, s_len, LANES), F32)],
        grid=(n_batch, n_chunks),
        in_specs=[
            pl.BlockSpec((1, CHUNK, cdim), lambda b, c: (b, c, 0)),
            pl.BlockSpec((1, HALO, cdim), lambda b, c: (b, jnp.maximum(c * per - 1, 0), 0)),
            pl.BlockSpec((1, HALO, cdim), lambda b, c: (b, jnp.minimum(c * per + per, last_halo), 0)),
            pl.BlockSpec((1, CHUNK, LANES), lambda b, c: (b, c, 0)),
            pl.BlockSpec((SSD_CONV, cdim), lambda b, c: (0, 0)),
            pl.BlockSpec((1, cdim), lambda b, c: (0, 0)),
            pl.BlockSpec((1, LANES), lambda b, c: (0, 0)),
            pl.BlockSpec((1, LANES), lambda b, c: (0, 0)),
        ],
        out_specs=[pl.BlockSpec((1, CHUNK, cdim), lambda b, c: (b, c, 0)),
                   pl.BlockSpec((1, CHUNK, LANES), lambda b, c: (b, c, 0))],
        scratch_shapes=[pltpu.VMEM((CHUNK + 16, cdim), F32)],
        compiler_params=_cparams(("parallel", "arbitrary")),
    )(xbc, xbc, xbc, dt_raw, conv_w, conv_b.reshape(1, cdim), dt_bias4, a_log4)


def _bwd_chunk(s, ctx_chunks, n_chunks):
    return jnp.where(s < ctx_chunks, ctx_chunks - 1 - s, n_chunks - 1 - (s - ctx_chunks))


def _ssd_chunk(xs_ref, b_ref, c_ref, dtla_ref, skip, state_ref, y_ref, *, reverse):
    n_heads = xs_ref.shape[2] // SSD_HEAD_DIM
    hp = xs_ref.shape[2]
    gw = hp // SSD_GROUPS
    off_dt = LANES // 4 if reverse else 0
    off_la = off_dt + LANES // 2

    row = lax.broadcasted_iota(jnp.int32, (CHUNK, CHUNK), 0)
    col = lax.broadcasted_iota(jnp.int32, (CHUNK, CHUNK), 1)
    before = (col >= row) if reverse else (col <= row)
    tri = jnp.where(before, 1.0, 0.0).astype(BF16)
    tri_t = jnp.where((row >= col) if reverse else (row <= col), 1.0, 0.0).astype(BF16)

    dtla = dtla_ref[0]
    hi, lo = _split_bf16(dtla)
    cum = _dot(tri, hi) + _dot(tri, lo)
    cum_t = _dot_tn(hi, tri_t) + _dot_tn(lo, tri_t)
    cum = jnp.where(lax.broadcasted_iota(jnp.int32, cum.shape, 1) >= LANES // 2, cum, 0.0)
    last = 0 if reverse else CHUNK - 1
    total = cum[last:last + 1, :]

    sec = lax.broadcasted_iota(jnp.int32, (LANES, hp), 0)
    head = lax.broadcasted_iota(jnp.int32, (LANES, hp), 1) // SSD_HEAD_DIM
    e_dt = jnp.where(sec == head + off_dt, 1.0, 0.0).astype(BF16)
    e_la = jnp.where(sec == head + off_la, 1.0, 0.0).astype(BF16)

    xs = xs_ref[0].astype(F32)
    xdt = xs * _dot_split_lhs(dtla, e_dt)
    xdt_b = xdt.astype(BF16)
    dec_in = _dot_split_lhs(jnp.exp(cum), e_la)
    dec_end = _dot_split_lhs(jnp.exp(total - cum), e_la)
    dec_all = _dot_split_lhs(jnp.exp(total), e_la)
    xdec_b = (xdt * dec_end).astype(BF16)

    cm = c_ref[0]
    bm = b_ref[0]
    lane = lax.broadcasted_iota(jnp.int32, (CHUNK, LANES), 1)
    low = lane < SSD_HEAD_DIM
    heads_per_group = n_heads // SSD_GROUPS
    for g in range(SSD_GROUPS):
        cg = cm[:, g * SSD_STATE:(g + 1) * SSD_STATE]
        bg = bm[:, g * SSD_STATE:(g + 1) * SSD_STATE]
        scores = _dot_nt(cg, bg)
        cols = slice(g * gw, (g + 1) * gw)
        h_enter = state_ref[:, cols]
        y_inter = _dot(cg, h_enter.astype(BF16)) * dec_in[:, cols]
        state_ref[:, cols] = h_enter * dec_all[:, cols] + _dot_tn(bg, xdec_b[:, cols])
        for pair in range(heads_per_group // 2):
            acc = None
            base = g * gw + pair * LANES
            rhs = xdt_b[:, base:base + LANES]
            for k in range(2):
                h = g * heads_per_group + pair * 2 + k
                seg = cum[:, off_la + h:off_la + h + 1] - cum_t[off_la + h:off_la + h + 1, :]
                m = (scores * jnp.exp(jnp.where(before, seg, NEG_BIG))).astype(BF16)
                keep = low if k == 0 else jnp.logical_not(low)
                part = _dot(m, jnp.where(keep, rhs, jnp.zeros_like(rhs)))
                acc = part if acc is None else acc + part
            y = acc + y_inter[:, pair * LANES:(pair + 1) * LANES]
            if skip is not None:
                y = y + xs[:, base:base + LANES] * skip[:, base:base + LANES]
            y_ref[0, :, base:base + LANES] = y.astype(y_ref.dtype)


def _ssd_scan_kernel(xs_f, b_f, c_f, dtla_f, xs_b, b_b, c_b, dtla_b, skip_ref,
                     yf_ref, yb_ref, st_f, st_b):
    @pl.when(pl.program_id(1) == 0)
    def _():
        st_f[...] = jnp.zeros_like(st_f)
        st_b[...] = jnp.zeros_like(st_b)

    _ssd_chunk(xs_f, b_f, c_f, dtla_f, skip_ref[...], st_f, yf_ref, reverse=False)
    _ssd_chunk(xs_b, b_b, c_b, dtla_b, None, st_b, yb_ref, reverse=True)


def _ssd_scan(xbc, dtla, skip, hp, ctx_chunks):
    n_batch, s_len, _ = xbc.shape
    n_chunks = s_len // CHUNK
    bc_w = SSD_GROUPS * SSD_STATE
    b_blk = hp // bc_w
    fwd = lambda s: s
    bwd = lambda s: _bwd_chunk(s, ctx_chunks, n_chunks)

    def specs(order):
        return [
            pl.BlockSpec((1, CHUNK, hp), lambda b, s: (b, order(s), 0)),
            pl.BlockSpec((1, CHUNK, bc_w), lambda b, s: (b, order(s), b_blk)),
            pl.BlockSpec((1, CHUNK, bc_w), lambda b, s: (b, order(s), b_blk + 1)),
            pl.BlockSpec((1, CHUNK, LANES), lambda b, s: (b, order(s), 0)),
        ]

    return pl.pallas_call(
        _ssd_scan_kernel,
        out_shape=[jax.ShapeDtypeStruct((n_batch, s_len, hp), F32)] * 2,
        grid=(n_batch, n_chunks),
        in_specs=specs(fwd) + specs(bwd) + [pl.BlockSpec((1, hp), lambda b, s: (0, 0))],
        out_specs=[pl.BlockSpec((1, CHUNK, hp), lambda b, s: (b, fwd(s), 0)),
                   pl.BlockSpec((1, CHUNK, hp), lambda b, s: (b, bwd(s), 0))],
        scratch_shapes=[pltpu.VMEM((SSD_STATE, hp), F32), pltpu.VMEM((SSD_STATE, hp), F32)],
        compiler_params=_cparams(("parallel", "arbitrary")),
    )(xbc, xbc, xbc, dtla, xbc, xbc, xbc, dtla, skip)


def _ret_chunk(q_ref, k_ref, v_ref, lg_ref, state_ref, y_ref, *, reverse):
    qk = q_ref.shape[2] // RET_HEADS
    vd = v_ref.shape[2] // RET_HEADS
    row = lax.broadcasted_iota(jnp.int32, (CHUNK, CHUNK), 0)
    col = lax.broadcasted_iota(jnp.int32, (CHUNK, CHUNK), 1)
    before = (col >= row) if reverse else (col <= row)
    dist = jnp.abs(row - col).astype(F32)
    n_in = ((CHUNK - row) if reverse else (row + 1)).astype(F32)
    n_end = ((row) if reverse else (CHUNK - 1 - row)).astype(F32)
    q = q_ref[0]
    k = k_ref[0]
    v = v_ref[0]
    d_sel = 1 if reverse else 0
    for h in range(RET_HEADS):
        lg = lg_ref[d_sel, h]
        qh = q[:, h * qk:(h + 1) * qk]
        kh = k[:, h * qk:(h + 1) * qk]
        vh = v[:, h * vd:(h + 1) * vd]
        scores = _dot_nt(qh, kh)
        m = (scores * jnp.exp(jnp.where(before, dist * lg, NEG_BIG))).astype(BF16)
        y_intra = _dot(m, vh)
        h_enter = state_ref[h]
        r_in = jnp.exp(n_in * lg)
        y_inter = _dot(qh, h_enter.astype(BF16))
        scale = jnp.concatenate([r_in] * (vd // CHUNK), axis=1)
        y_ref[0, :, h * vd:(h + 1) * vd] = (y_intra + y_inter * scale).astype(y_ref.dtype)
        r_end = jnp.exp(n_end * lg)
        k_scale = jnp.concatenate([r_end] * (qk // CHUNK), axis=1)
        kdec = (kh.astype(F32) * k_scale).astype(BF16)
        state_ref[h] = h_enter * jnp.exp(lg * CHUNK) + _dot_tn(kdec, vh)


def _ret_scan_kernel(lg_ref, q_f, k_f, v_f, q_b, k_b, v_b, yf_ref, yb_ref, st_f, st_b):
    @pl.when(pl.program_id(1) == 0)
    def _():
        st_f[...] = jnp.zeros_like(st_f)
        st_b[...] = jnp.zeros_like(st_b)

    _ret_chunk(q_f, k_f, v_f, lg_ref, st_f, yf_ref, reverse=False)
    _ret_chunk(q_b, k_b, v_b, lg_ref, st_b, yb_ref, reverse=True)


def _ret_scan(q, k, v, log_gamma, ctx_chunks):
    n_batch, s_len, qk_tot = q.shape
    v_tot = v.shape[2]
    n_chunks = s_len // CHUNK
    fwd = lambda s: s
    bwd = lambda s: _bwd_chunk(s, ctx_chunks, n_chunks)

    def specs(order):
        return [
            pl.BlockSpec((1, CHUNK, qk_tot), lambda b, s: (b, order(s), 0)),
            pl.BlockSpec((1, CHUNK, qk_tot), lambda b, s: (b, order(s), 0)),
            pl.BlockSpec((1, CHUNK, v_tot), lambda b, s: (b, order(s), 0)),
        ]

    return pl.pallas_call(
        _ret_scan_kernel,
        out_shape=[jax.ShapeDtypeStruct((n_batch, s_len, v_tot), F32)] * 2,
        grid=(n_batch, n_chunks),
        in_specs=[pl.BlockSpec(memory_space=pltpu.SMEM)] + specs(fwd) + specs(bwd),
        out_specs=[pl.BlockSpec((1, CHUNK, v_tot), lambda b, s: (b, fwd(s), 0)),
                   pl.BlockSpec((1, CHUNK, v_tot), lambda b, s: (b, bwd(s), 0))],
        scratch_shapes=[pltpu.VMEM((RET_HEADS, qk_tot // RET_HEADS, v_tot // RET_HEADS), F32)] * 2,
        compiler_params=_cparams(("parallel", "arbitrary")),
    )(log_gamma, q, k, v, q, k, v)


def _readout_kernel(yf_ref, yb_ref, gate_ref, ng_ref, w_ref, x_ref, mod_ref, out_ref, *,
                    gate_first, group, gate_idx):
    y = yf_ref[0] + yb_ref[0]
    gate = _silu(gate_ref[0].astype(F32))
    if gate_first:
        y = y * gate
    pieces = []
    for j in range(y.shape[1] // group):
        yg = y[:, j * group:(j + 1) * group]
        ms = jnp.mean(yg * yg, axis=-1, keepdims=True)
        pieces.append(yg * lax.rsqrt(ms + NORM_EPS))
    y = jnp.concatenate(pieces, axis=1)
    y = y * ng_ref[...] if gate_first else y * gate
    o = _dot(y.astype(BF16), w_ref[...])
    out_ref[0] = x_ref[0] + mod_ref[0][gate_idx:gate_idx + 1] * o


def _readout(y_f, y_b, gate, norm_g, w_out, x, mod, ctx_tiles, gate_first, group):
    n_batch, s_len, inner = y_f.shape
    d = x.shape[2]
    n_tiles = s_len // ROW_TILE
    row = lambda b, t: (b, t, 0)
    return pl.pallas_call(
        functools.partial(_readout_kernel, gate_first=gate_first, group=group, gate_idx=2),
        out_shape=jax.ShapeDtypeStruct(x.shape, F32),
        grid=(n_batch, n_tiles),
        in_specs=[
            pl.BlockSpec((1, ROW_TILE, inner), row),
            pl.BlockSpec((1, ROW_TILE, inner), row),
            pl.BlockSpec((1, ROW_TILE, inner), row),
            pl.BlockSpec((1, inner), lambda b, t: (0, 0)),
            pl.BlockSpec((inner, d), lambda b, t: (0, 0)),
            pl.BlockSpec((1, ROW_TILE, d), row),
            pl.BlockSpec((1, 6, d), lambda b, t: (jnp.where(t < ctx_tiles, n_batch, b), 0, 0)),
        ],
        out_specs=pl.BlockSpec((1, ROW_TILE, d), row),
        compiler_params=_cparams(("parallel", "arbitrary")),
    )(y_f, y_b, gate, norm_g.reshape(1, inner), w_out, x, mod)


def _router_kernel(x_ref, g_ref, mod_ref, wr_ref, br_ref, h_ref, idx_ref, gate_ref):
    h = _norm_modulate(x_ref[0], g_ref[...], mod_ref[0], 3, 4)
    h_ref[0] = h
    h_hi, h_lo = _split_bf16(h)
    w_hi, w_lo = _split_bf16(wr_ref[...])
    logits = _dot(h_hi, w_hi) + _dot(h_lo, w_hi) + _dot(h_hi, w_lo) + br_ref[...]
    lane = lax.broadcasted_iota(jnp.int32, logits.shape, 1).astype(F32)
    work = logits
    idx_out = jnp.zeros(logits.shape, F32)
    val_out = jnp.full(logits.shape, NEG_BIG, F32)
    for k in range(TOP_K):
        m = jnp.max(work, axis=-1, keepdims=True)
        first = jnp.min(jnp.where(work == m, lane, float(LANES)), axis=-1, keepdims=True)
        idx_out = jnp.where(lane == k, first, idx_out)
        val_out = jnp.where(lane == k, m, val_out)
        work = jnp.where(lane == first, -jnp.inf, work)
    top = jnp.max(val_out, axis=-1, keepdims=True)
    e = jnp.exp(val_out - top)
    idx_ref[0] = idx_out.astype(jnp.int32)
    gate_ref[0] = e / jnp.sum(e, axis=-1, keepdims=True)


def _router(x, g, mod, w_router, b_router, ctx_tiles):
    n_batch, s_len, d = x.shape
    n_exp = w_router.shape[1]
    wr = jnp.zeros((d, LANES), F32).at[:, :n_exp].set(w_router)
    br = jnp.full((1, LANES), NEG_BIG, F32).at[0, :n_exp].set(b_router)
    row = lambda b, t: (b, t, 0)
    return pl.pallas_call(
        _router_kernel,
        out_shape=[jax.ShapeDtypeStruct((n_batch, s_len, d), F32),
                   jax.ShapeDtypeStruct((n_batch, s_len, LANES), jnp.int32),
                   jax.ShapeDtypeStruct((n_batch, s_len, LANES), F32)],
        grid=(n_batch, s_len // ROW_TILE),
        in_specs=[
            pl.BlockSpec((1, ROW_TILE, d), row),
            pl.BlockSpec((1, d), lambda b, t: (0, 0)),
            pl.BlockSpec((1, 6, d), lambda b, t: (jnp.where(t < ctx_tiles, n_batch, b), 0, 0)),
            pl.BlockSpec((d, LANES), lambda b, t: (0, 0)),
            pl.BlockSpec((1, LANES), lambda b, t: (0, 0)),
        ],
        out_specs=[pl.BlockSpec((1, ROW_TILE, d), row),
                   pl.BlockSpec((1, ROW_TILE, LANES), row),
                   pl.BlockSpec((1, ROW_TILE, LANES), row)],
        compiler_params=_cparams(("parallel", "arbitrary")),
    )(x, g.reshape(1, d), mod, wr, br)


def _expert_kernel(bexp_ref, nused_ref, rows_hbm, h_hbm, gate_ref, w1g_ref, w1l_ref, b1g_ref,
                   b1l_ref, w2_ref, b2_ref, out_ref, xbuf, idx_a, idx_b, gsem, isem):
    i = pl.program_id(0)
    n = pl.num_programs(0)
    idx = (idx_a, idx_b)

    def idx_copy(block, slot):
        return pltpu.make_async_copy(rows_hbm.at[block], idx[slot], isem.at[slot])

    def gather_issue(slot):
        def body(r, carry):
            tok = idx[slot][r]
            pltpu.make_async_copy(h_hbm.at[pl.ds(tok, 1), :], xbuf.at[slot, pl.ds(r, 1), :],
                                  gsem.at[slot]).start()
            return carry
        lax.fori_loop(0, EXPERT_ROWS, body, 0)

    def gather_wait(slot):
        pltpu.make_async_copy(h_hbm.at[pl.ds(0, EXPERT_ROWS), :], xbuf.at[slot], gsem.at[slot]).wait()

    def on_slot(parity_of, fn):
        for s in range(2):
            @pl.when(parity_of % 2 == s)
            def _(s=s):
                fn(s)

    @pl.when(i == 0)
    def _():
        idx_copy(0, 0).start()
        idx_copy(0, 0).wait()
        gather_issue(0)

        @pl.when(n > 1)
        def _():
            idx_copy(1, 1).start()

    @pl.when(i + 1 < n)
    def _():
        def nxt(s):
            idx_copy(i + 1, s).wait()
            gather_issue(s)
        on_slot(i + 1, nxt)

    @pl.when(i + 2 < n)
    def _():
        on_slot(i + 2, lambda s: idx_copy(i + 2, s).start())

    on_slot(i, gather_wait)

    @pl.when(i < nused_ref[0])
    def _():
        def compute(s):
            x = xbuf[s].astype(BF16)
            hg = _dot(x, w1g_ref[0]) + b1g_ref[0]
            hl = _dot(x, w1l_ref[0]) + b1l_ref[0]
            glu = jnp.minimum(hg, SWIGLU_LIMIT)
            lin = jnp.clip(hl, -SWIGLU_LIMIT, SWIGLU_LIMIT)
            act = glu * _sigmoid(SWIGLU_ALPHA * glu) * (lin + 1.0)
            y = _dot(act.astype(BF16), w2_ref[0]) + b2_ref[0]
            out_ref[...] = y * gate_ref[...]
        on_slot(i, compute)

    @pl.when(i >= nused_ref[0])
    def _():
        out_ref[...] = jnp.zeros_like(out_ref)


def _experts(h_flat, row_tok, row_gate, block_expert, n_used, w1g, w1l, b1g, b1l, w2, b2):
    n_blocks = block_expert.shape[0]
    d = h_flat.shape[1]
    f = w1g.shape[2]
    wspec = lambda shape: pl.BlockSpec(shape, lambda i, be, nu: (be[i], 0, 0))
    grid_spec = pltpu.PrefetchScalarGridSpec(
        num_scalar_prefetch=2,
        grid=(n_blocks,),
        in_specs=[
            pl.BlockSpec(memory_space=pl.ANY),
            pl.BlockSpec(memory_space=pl.ANY),
            pl.BlockSpec((EXPERT_ROWS, 1), lambda i, be, nu: (i, 0)),
            wspec((1, d, f)), wspec((1, d, f)), wspec((1, 1, f)), wspec((1, 1, f)),
            wspec((1, f, d)), wspec((1, 1, d)),
        ],
        out_specs=pl.BlockSpec((EXPERT_ROWS, d), lambda i, be, nu: (i, 0)),
        scratch_shapes=[
            pltpu.VMEM((2, EXPERT_ROWS, d), F32),
            pltpu.SMEM((EXPERT_ROWS,), jnp.int32),
            pltpu.SMEM((EXPERT_ROWS,), jnp.int32),
            pltpu.SemaphoreType.DMA((2,)),
            pltpu.SemaphoreType.DMA((2,)),
        ],
    )
    return pl.pallas_call(
        _expert_kernel,
        out_shape=jax.ShapeDtypeStruct((n_blocks * EXPERT_ROWS, d), F32),
        grid_spec=grid_spec,
        compiler_params=_cparams(("arbitrary",)),
    )(block_expert, n_used, row_tok.reshape(n_blocks, EXPERT_ROWS), h_flat,
      row_gate.reshape(-1, 1), w1g, w1l, b1g, b1l, w2, b2)


def _combine_kernel(pos_hbm, y_hbm, x_ref, mod_ref, out_ref, ybuf, idx_a, idx_b, gsem, isem):
    i = pl.program_id(0)
    n = pl.num_programs(0)
    idx = (idx_a, idx_b)
    n_rows = TOP_K * ROW_TILE

    def idx_copy(block, slot):
        return pltpu.make_async_copy(pos_hbm.at[block], idx[slot], isem.at[slot])

    def gather_issue(slot):
        def body(r, carry):
            p = idx[slot][r]
            pltpu.make_async_copy(y_hbm.at[pl.ds(p, 1), :], ybuf.at[slot, pl.ds(r, 1), :],
                                  gsem.at[slot]).start()
            return carry
        lax.fori_loop(0, n_rows, body, 0)

    def gather_wait(slot):
        pltpu.make_async_copy(y_hbm.at[pl.ds(0, n_rows), :], ybuf.at[slot], gsem.at[slot]).wait()

    def on_slot(parity_of, fn):
        for s in range(2):
            @pl.when(parity_of % 2 == s)
            def _(s=s):
                fn(s)

    @pl.when(i == 0)
    def _():
        idx_copy(0, 0).start()
        idx_copy(0, 0).wait()
        gather_issue(0)

        @pl.when(n > 1)
        def _():
            idx_copy(1, 1).start()

    @pl.when(i + 1 < n)
    def _():
        def nxt(s):
            idx_copy(i + 1, s).wait()
            gather_issue(s)
        on_slot(i + 1, nxt)

    @pl.when(i + 2 < n)
    def _():
        on_slot(i + 2, lambda s: idx_copy(i + 2, s).start())

    on_slot(i, gather_wait)

    def compute(s):
        acc = ybuf[s, 0:ROW_TILE, :]
        for k in range(1, TOP_K):
            acc = acc + ybuf[s, k * ROW_TILE:(k + 1) * ROW_TILE, :]
        out_ref[...] = x_ref[...] + mod_ref[0][5:6] * acc
    on_slot(i, compute)


def _combine(pos_tiles, y_rows, x_flat, mod, n_batch, tiles_per_batch, ctx_tiles):
    n_tiles = pos_tiles.shape[0]
    d = x_flat.shape[1]

    def mod_map(i):
        b = i // tiles_per_batch
        return (jnp.where(i % tiles_per_batch < ctx_tiles, n_batch, b), 0, 0)

    return pl.pallas_call(
        _combine_kernel,
        out_shape=jax.ShapeDtypeStruct(x_flat.shape, F32),
        grid=(n_tiles,),
        in_specs=[
            pl.BlockSpec(memory_space=pl.ANY),
            pl.BlockSpec(memory_space=pl.ANY),
            pl.BlockSpec((ROW_TILE, d), lambda i: (i, 0)),
            pl.BlockSpec((1, 6, d), mod_map),
        ],
        out_specs=pl.BlockSpec((ROW_TILE, d), lambda i: (i, 0)),
        scratch_shapes=[
            pltpu.VMEM((2, TOP_K * ROW_TILE, d), F32),
            pltpu.SMEM((TOP_K * ROW_TILE,), jnp.int32),
            pltpu.SMEM((TOP_K * ROW_TILE,), jnp.int32),
            pltpu.SemaphoreType.DMA((2,)),
            pltpu.SemaphoreType.DMA((2,)),
        ],
        compiler_params=_cparams(("arbitrary",)),
    )(pos_tiles, y_rows, x_flat, mod)


def _routing_tables(top_i, gates, n_experts):
    n_tok = top_i.shape[0]
    n_assign = n_tok * TOP_K
    e_flat = top_i.reshape(-1)
    order = jnp.argsort(e_flat, stable=True)
    e_sorted = e_flat[order]
    counts = jnp.bincount(e_flat, length=n_experts)
    padded = (counts + EXPERT_ROWS - 1) // EXPERT_ROWS * EXPERT_ROWS
    start = jnp.cumsum(counts) - counts
    pend = jnp.cumsum(padded)
    pstart = pend - padded
    dest = (pstart[e_sorted] + (jnp.arange(n_assign) - start[e_sorted])).astype(jnp.int32)
    n_blocks = -(-n_assign // EXPERT_ROWS) + n_experts
    cap = n_blocks * EXPERT_ROWS
    row_tok = jnp.zeros((cap,), jnp.int32).at[dest].set((order // TOP_K).astype(jnp.int32))
    row_gate = jnp.zeros((cap,), F32).at[dest].set(gates.reshape(-1)[order])
    block_expert = jnp.minimum(
        jnp.searchsorted(pend, jnp.arange(n_blocks) * EXPERT_ROWS, side='right'),
        n_experts - 1).astype(jnp.int32)
    n_used = (pend[-1] // EXPERT_ROWS).astype(jnp.int32).reshape(1)
    pos = jnp.zeros((n_assign,), jnp.int32).at[order].set(dest)
    return row_tok, row_gate, block_expert, n_used, pos


def _moe(x, g, mod, w_router, b_router, w1g, w1l, b1g, b1l, w2, b2, ctx_tiles):
    n_batch, s_len, d = x.shape
    n_tok = n_batch * s_len
    h, idx, gate = _router(x, g, mod, w_router, b_router, ctx_tiles)
    top_i = idx[..., :TOP_K].reshape(n_tok, TOP_K)
    gates = gate[..., :TOP_K].reshape(n_tok, TOP_K)
    row_tok, row_gate, block_expert, n_used, pos = _routing_tables(top_i, gates, w1g.shape[0])
    y_rows = _experts(h.reshape(n_tok, d), row_tok, row_gate, block_expert, n_used,
                      w1g, w1l, b1g, b1l, w2, b2)
    pos_tiles = pos.reshape(n_tok // ROW_TILE, ROW_TILE, TOP_K).transpose(0, 2, 1)
    pos_tiles = pos_tiles.reshape(n_tok // ROW_TILE, TOP_K * ROW_TILE)
    out = _combine(pos_tiles, y_rows, x.reshape(n_tok, d), mod, n_batch, s_len // ROW_TILE, ctx_tiles)
    return out.reshape(n_batch, s_len, d)


def _final_norm_kernel(x_ref, g_ref, out_ref):
    x = x_ref[0]
    ms = jnp.mean(x * x, axis=-1, keepdims=True)
    out_ref[0] = x * lax.rsqrt(ms + NORM_EPS) * g_ref[...]


def _final_norm(x, g, ctx_tiles, seq_len):
    n_batch, _, d = x.shape
    return pl.pallas_call(
        _final_norm_kernel,
        out_shape=jax.ShapeDtypeStruct((n_batch, seq_len, d), F32),
        grid=(n_batch, seq_len // ROW_TILE),
        in_specs=[pl.BlockSpec((1, ROW_TILE, d), lambda b, t: (b, t + ctx_tiles, 0)),
                  pl.BlockSpec((1, d), lambda b, t: (0, 0))],
        out_specs=pl.BlockSpec((1, ROW_TILE, d), lambda b, t: (b, t, 0)),
        compiler_params=_cparams(("parallel", "arbitrary")),
    )(x, g.reshape(1, d))


def _rope_tables(seq_len, ctx_len, half_dim):
    rows = seq_len // GRID_W
    t = jnp.arange(seq_len)
    row_pos = (t // GRID_W).astype(F32) - (rows - 1) / 2.0
    col_pos = (t % GRID_W).astype(F32) - (GRID_W - 1) / 2.0
    inv_freq = ROPE_BASE ** (-jnp.arange(0, half_dim, 2, dtype=F32) / half_dim)

    def tables(pos):
        ang = pos[:, None] * inv_freq[None, :]
        c, s = jnp.cos(ang), jnp.sin(ang)
        return jnp.concatenate([c, c], axis=1), jnp.concatenate([-s, s], axis=1)

    cr, sr = tables(row_pos)
    cc, sc = tables(col_pos)
    cos = jnp.concatenate([cr, cc], axis=1)
    sin = jnp.concatenate([sr, sc], axis=1)
    cos = jnp.concatenate([jnp.ones((ctx_len, cos.shape[1]), F32), cos], axis=0)
    sin = jnp.concatenate([jnp.zeros((ctx_len, sin.shape[1]), F32), sin], axis=0)
    return cos, sin


def kernel(x, c, ctx, c_ctx, w_mod, b_mod, norm1_g, norm2_g, ssd_w_in, ssd_conv_w, ssd_conv_b,
           ssd_dt_bias_f, ssd_dt_bias_b, ssd_a_log_f, ssd_a_log_b, ssd_d, ssd_norm_g, ssd_w_out,
           ret_w_in, ret_scale_f, ret_scale_b, ret_w_out, router_w, router_b,
           moe_w1, moe_b1, moe_w2, moe_b2, final_g):
    n_batch, seq_len, d = x.shape
    ctx_len = ctx.shape[1]
    depth = w_mod.shape[0]
    assert ctx_len % ROW_TILE == 0 and seq_len % ROW_TILE == 0
    ctx_tiles = ctx_len // ROW_TILE
    ctx_chunks = ctx_len // CHUNK

    n_heads = ssd_d.shape[1]
    ssd_inner = n_heads * SSD_HEAD_DIM
    bc_w = SSD_GROUPS * SSD_STATE
    conv_dim = ssd_inner + 2 * bc_w
    assert n_heads == LANES // 4 and ssd_inner % bc_w == 0

    ret_qk = (ret_w_in.shape[2] - 2 * ret_w_out.shape[1]) // 2
    ret_v = ret_w_out.shape[1]
    qk_dim = ret_qk // RET_HEADS
    assert qk_dim == 2 * LANES

    cond = jnp.concatenate([c, c_ctx[None, :], jnp.zeros((16 - n_batch - 1, d), F32)], axis=0)
    mod_all = _modulation(cond, w_mod, b_mod).reshape(depth, 16, 6, d)

    xs = jnp.concatenate([ctx, x], axis=1)
    cos, sin = _rope_tables(seq_len, ctx_len, qk_dim // 2)

    for i in range(depth):
        j = i // 2
        mod = mod_all[i]
        if i % 2 == 0:
            w_in = ssd_w_in[j]
            w_main = w_in[:, :ssd_inner + conv_dim].astype(BF16)
            w_dt = w_in[:, ssd_inner + conv_dim:].astype(BF16)
            w_cat = jnp.concatenate([w_main, w_dt, w_dt], axis=1)
            splits = [(0, ssd_inner), (ssd_inner, conv_dim), (ssd_inner + conv_dim, LANES)]
            z, xbc, dt_raw = _in_proj(xs, norm1_g[i], mod, w_cat, splits, [BF16, BF16, F32], ctx_tiles)
            dt_bias2 = jnp.concatenate([ssd_dt_bias_f[j], ssd_dt_bias_b[j]])
            dt_bias4 = jnp.concatenate([dt_bias2, dt_bias2]).reshape(1, LANES)
            a_log4 = jnp.concatenate([jnp.zeros((2 * n_heads,), F32), ssd_a_log_f[j],
                                      ssd_a_log_b[j]]).reshape(1, LANES)
            xbc_c, dtla = _ssd_conv(xbc, dt_raw, ssd_conv_w[j], ssd_conv_b[j], dt_bias4, a_log4,
                                    ctx_chunks)
            skip = jnp.repeat(ssd_d[j], SSD_HEAD_DIM).reshape(1, ssd_inner)
            y_f, y_b = _ssd_scan(xbc_c, dtla, skip, ssd_inner, ctx_chunks)
            xs = _readout(y_f, y_b, z, ssd_norm_g[j], ssd_w_out[j].astype(BF16), xs, mod,
                          ctx_tiles, True, ssd_inner // SSD_GROUPS)
        else:
            w_cat = ret_w_in[j].astype(BF16)
            splits = [(0, ret_qk), (ret_qk, ret_qk), (2 * ret_qk, ret_v), (2 * ret_qk + ret_v, ret_v)]
            q, k, v, gt = _in_proj(xs, norm1_g[i], mod, w_cat, splits, [BF16] * 4, ctx_tiles,
                                   rope=(cos, sin, (0, 1)), k_scale=qk_dim ** -0.5)
            log_gamma = jnp.stack([jnp.log1p(-jnp.exp2(-ret_scale_f[j].astype(F32))),
                                   jnp.log1p(-jnp.exp2(-ret_scale_b[j].astype(F32)))])
            y_f, y_b = _ret_scan(q, k, v, log_gamma, ctx_chunks)
            xs = _readout(y_f, y_b, gt, jnp.ones((ret_v,), F32), ret_w_out[j].astype(BF16), xs, mod,
                          ctx_tiles, False, ret_v // RET_HEADS)
        w1 = moe_w1[i]
        b1 = moe_b1[i]
        n_exp = w1.shape[0]
        xs = _moe(xs, norm2_g[i], mod, router_w[i], router_b[i],
                  w1[:, :, 0::2].astype(BF16), w1[:, :, 1::2].astype(BF16),
                  b1[:, 0::2].reshape(n_exp, 1, -1), b1[:, 1::2].reshape(n_exp, 1, -1),
                  moe_w2[i].astype(BF16), moe_b2[i].reshape(n_exp, 1, -1), ctx_tiles)
    return _final_norm(xs, final_g, ctx_tiles, seq_len)
```

```python
import functools

import jax
import jax.numpy as jnp
from jax import lax
from jax.experimental import pallas as pl
from jax.experimental.pallas import tpu as pltpu

F32 = jnp.float32
BF16 = jnp.bfloat16

NORM_EPS = 1e-6
CHUNK = 128
GRID_W = 64
ROPE_BASE = 10000.0

SSD_HEAD_DIM = 64
SSD_GROUPS = 4
SSD_STATE = 128
SSD_CONV = 5

RET_HEADS = 4

TOP_K = 4
SWIGLU_ALPHA = 1.702
SWIGLU_LIMIT = 7.0

LANES = 128
MXU_DIM = 256
ROW_TILE = 256
EXPERT_ROWS = 256
HALO = 16
DMA_UNROLL = 8
VMEM_LIMIT = 56 * 1024 * 1024
NEG_BIG = -1e30


def _cparams(sem):
    return pltpu.CompilerParams(dimension_semantics=sem, vmem_limit_bytes=VMEM_LIMIT)


def _split_bf16(a):
    hi = a.astype(BF16)
    lo = (a - hi.astype(F32)).astype(BF16)
    return hi, lo


def _dot(a, b):
    return jnp.dot(a, b, preferred_element_type=F32)


def _dot_tn(a, b):
    return lax.dot_general(a, b, (((0,), (0,)), ((), ())), preferred_element_type=F32)


def _dot_nt(a, b):
    return lax.dot_general(a, b, (((1,), (1,)), ((), ())), preferred_element_type=F32)


def _dot_split_lhs(a, b_bf16):
    hi, lo = _split_bf16(a)
    return _dot(hi, b_bf16) + _dot(lo, b_bf16)


def _sigmoid(x):
    return 1.0 / (1.0 + jnp.exp(-x))


def _silu(x):
    return x * _sigmoid(x)


def _mod_kernel(cond_ref, w_ref, b_ref, out_ref):
    s = _silu(cond_ref[...])
    s_hi, s_lo = _split_bf16(s)
    w_hi, w_lo = _split_bf16(w_ref[0])
    acc = _dot(s_hi, w_hi) + _dot(s_lo, w_hi) + _dot(s_hi, w_lo)
    out_ref[0] = acc + b_ref[0]


def _modulation(cond, w_mod, b_mod):
    depth, d, n = w_mod.shape
    rows = cond.shape[0]
    tn = 1024
    return pl.pallas_call(
        _mod_kernel,
        out_shape=jax.ShapeDtypeStruct((depth, rows, n), F32),
        grid=(depth, n // tn),
        in_specs=[
            pl.BlockSpec((rows, d), lambda l, j: (0, 0)),
            pl.BlockSpec((1, d, tn), lambda l, j: (l, 0, j)),
            pl.BlockSpec((1, 1, tn), lambda l, j: (l, 0, j)),
        ],
        out_specs=pl.BlockSpec((1, rows, tn), lambda l, j: (l, 0, j)),
        compiler_params=_cparams(("arbitrary", "arbitrary")),
    )(cond, w_mod, b_mod.reshape(depth, 1, n))


def _norm_modulate(x, g, mod, shift_idx, scale_idx):
    ms = jnp.mean(x * x, axis=-1, keepdims=True)
    h = x * lax.rsqrt(ms + NORM_EPS) * g
    return h * (1.0 + mod[scale_idx:scale_idx + 1]) + mod[shift_idx:shift_idx + 1]


def _in_proj_kernel(*refs, splits, rope_outs, k_scale):
    if rope_outs:
        x_ref, g_ref, mod_ref, w_ref, cos_ref, sin_ref = refs[:6]
        out_refs = refs[6:]
    else:
        x_ref, g_ref, mod_ref, w_ref = refs[:4]
        out_refs = refs[4:]
    h = _norm_modulate(x_ref[0], g_ref[...], mod_ref[0], 0, 1).astype(BF16)
    for idx, (start, width) in enumerate(splits):
        o = _dot(h, w_ref[:, start:start + width])
        if idx in rope_outs:
            cos = cos_ref[...]
            sin = sin_ref[...]
            pieces = []
            for j in range(width // LANES):
                u = o[:, j * LANES:(j + 1) * LANES]
                half = (j % 2) * LANES
                r = (u * cos[:, half:half + LANES]
                     + pltpu.roll(u, LANES // 2, axis=1) * sin[:, half:half + LANES])
                pieces.append(r)
            o = jnp.concatenate(pieces, axis=1)
            if idx == rope_outs[-1]:
                o = o * k_scale
        out_refs[idx][0] = o.astype(out_refs[idx].dtype)


def _in_proj(x, g, mod, w, splits, out_dtypes, ctx_tiles, rope=None, k_scale=1.0):
    n_batch, s_len, d = x.shape
    n_tiles = s_len // ROW_TILE
    in_specs = [
        pl.BlockSpec((1, ROW_TILE, d), lambda b, t: (b, t, 0)),
        pl.BlockSpec((1, d), lambda b, t: (0, 0)),
        pl.BlockSpec((1, 6, d), lambda b, t: (jnp.where(t < ctx_tiles, n_batch, b), 0, 0)),
        pl.BlockSpec(w.shape, lambda b, t: (0, 0)),
    ]
    args = [x, g.reshape(1, d), mod, w]
    rope_outs = ()
    if rope is not None:
        cos, sin, rope_outs = rope
        in_specs += [pl.BlockSpec((ROW_TILE, cos.shape[1]), lambda b, t: (t, 0)),
                     pl.BlockSpec((ROW_TILE, sin.shape[1]), lambda b, t: (t, 0))]
        args += [cos, sin]
    out_shape = [jax.ShapeDtypeStruct((n_batch, s_len, wd), dt)
                 for (_, wd), dt in zip(splits, out_dtypes)]
    out_specs = [pl.BlockSpec((1, ROW_TILE, wd), lambda b, t: (b, t, 0)) for _, wd in splits]
    return pl.pallas_call(
        functools.partial(_in_proj_kernel, splits=tuple(splits), rope_outs=tuple(rope_outs),
                          k_scale=k_scale),
        out_shape=out_shape,
        grid=(n_batch, n_tiles),
        in_specs=in_specs,
        out_specs=out_specs,
        compiler_params=_cparams(("parallel", "arbitrary")),
    )(*args)


def _ssd_conv_kernel(cur_ref, prev_ref, next_ref, dt_ref, cw_ref, cb_ref, dtb_ref, alog_ref,
                     xbc_ref, dtla_ref, ext_ref, *, ctx_chunks, n_chunks):
    c = pl.program_id(1)
    has_prev = jnp.logical_and(c != 0, c != ctx_chunks)
    has_next = jnp.logical_and(c != ctx_chunks - 1, c != n_chunks - 1)
    pad = SSD_CONV // 2
    ext_ref[0:8, :] = prev_ref[0].astype(F32)[HALO - 8:HALO, :] * jnp.where(has_prev, 1.0, 0.0)
    ext_ref[8:8 + CHUNK, :] = cur_ref[0].astype(F32)
    ext_ref[8 + CHUNK:16 + CHUNK, :] = next_ref[0].astype(F32)[0:8, :] * jnp.where(has_next, 1.0, 0.0)
    cw = cw_ref[...]
    acc = jnp.zeros((CHUNK, cur_ref.shape[2]), F32) + cb_ref[...]
    for k in range(SSD_CONV):
        acc = acc + ext_ref[8 - pad + k:8 - pad + k + CHUNK, :] * cw[k:k + 1, :]
    xbc_ref[0] = _silu(acc).astype(xbc_ref.dtype)
    raw = dt_ref[0] + dtb_ref[...]
    dt = jnp.maximum(raw, 0.0) + jnp.log(1.0 + jnp.exp(-jnp.abs(raw)))
    lane = lax.broadcasted_iota(jnp.int32, dt.shape, 1)
    a = -jnp.exp(alog_ref[...])
    dtla_ref[0] = jnp.where(lane < LANES // 2, dt, dt * a)


def _ssd_conv(xbc, dt_raw, conv_w, conv_b, dt_bias4, a_log4, ctx_chunks):
    n_batch, s_len, cdim = xbc.shape
    n_chunks = s_len // CHUNK
    per = CHUNK // HALO
    last_halo = s_len // HALO - 1
    return pl.pallas_call(
        functools.partial(_ssd_conv_kernel, ctx_chunks=ctx_chunks, n_chunks=n_chunks),
        out_shape=[jax.ShapeDtypeStruct((n_batch, s_len, cdim), BF16),
                   jax.ShapeDtypeStruct((n_batch, s_len, LANES), F32)],
        grid=(n_batch, n_chunks),
        in_specs=[
            pl.BlockSpec((1, CHUNK, cdim), lambda b, c: (b, c, 0)),
            pl.BlockSpec((1, HALO, cdim), lambda b, c: (b, jnp.maximum(c * per - 1, 0), 0)),
            pl.BlockSpec((1, HALO, cdim), lambda b, c: (b, jnp.minimum(c * per + per, last_halo), 0)),
            pl.BlockSpec((1, CHUNK, LANES), lambda b, c: (b, c, 0)),
            pl.BlockSpec((SSD_CONV, cdim), lambda b, c: (0, 0)),
            pl.BlockSpec((1, cdim), lambda b, c: (0, 0)),
            pl.BlockSpec((1, LANES), lambda b, c: (0, 0)),
            pl.BlockSpec((1, LANES), lambda b, c: (0, 0)),
        ],
        out_specs=[pl.BlockSpec((1, CHUNK, cdim), lambda b, c: (b, c, 0)),
                   pl.BlockSpec((1, CHUNK, LANES), lambda b, c: (b, c, 0))],
        scratch_shapes=[pltpu.VMEM((CHUNK + 16, cdim), F32)],
        compiler_params=_cparams(("parallel", "arbitrary")),
    )(xbc, xbc, xbc, dt_raw, conv_w, conv_b.reshape(1, cdim), dt_bias4, a_log4)


def _bwd_chunk(s, ctx_chunks, n_chunks):
    return jnp.where(s < ctx_chunks, ctx_chunks - 1 - s, n_chunks - 1 - (s - ctx_chunks))


def _ssd_chunk(xs_ref, b_ref, c_ref, dtla_ref, skip, state_ref, y_ref, *, reverse):
    n_heads = xs_ref.shape[2] // SSD_HEAD_DIM
    hp = xs_ref.shape[2]
    gw = hp // SSD_GROUPS
    off_dt = LANES // 4 if reverse else 0
    off_la = off_dt + LANES // 2

    row = lax.broadcasted_iota(jnp.int32, (CHUNK, CHUNK), 0)
    col = lax.broadcasted_iota(jnp.int32, (CHUNK, CHUNK), 1)
    before = (col >= row) if reverse else (col <= row)
    tri = jnp.where(before, 1.0, 0.0).astype(BF16)
    tri_t = jnp.where((row >= col) if reverse else (row <= col), 1.0, 0.0).astype(BF16)

    dtla = dtla_ref[0]
    hi, lo = _split_bf16(dtla)
    cum = _dot(tri, hi) + _dot(tri, lo)
    cum_t = _dot_tn(hi, tri_t) + _dot_tn(lo, tri_t)
    cum = jnp.where(lax.broadcasted_iota(jnp.int32, cum.shape, 1) >= LANES // 2, cum, 0.0)
    last = 0 if reverse else CHUNK - 1
    total = cum[last:last + 1, :]

    sec = lax.broadcasted_iota(jnp.int32, (LANES, hp), 0)
    head = lax.broadcasted_iota(jnp.int32, (LANES, hp), 1) // SSD_HEAD_DIM
    e_dt = jnp.where(sec == head + off_dt, 1.0, 0.0).astype(BF16)
    e_la = jnp.where(sec == head + off_la, 1.0, 0.0).astype(BF16)

    xs = xs_ref[0].astype(F32)
    xdt = xs * _dot_split_lhs(dtla, e_dt)
    xdt_b = xdt.astype(BF16)
    dec_in = _dot_split_lhs(jnp.exp(cum), e_la)
    dec_end = _dot_split_lhs(jnp.exp(total - cum), e_la)
    dec_all = _dot_split_lhs(jnp.exp(total), e_la)
    xdec_b = (xdt * dec_end).astype(BF16)

    cm = c_ref[0]
    bm = b_ref[0]
    lane = lax.broadcasted_iota(jnp.int32, (CHUNK, LANES), 1)
    low = lane < SSD_HEAD_DIM
    heads_per_group = n_heads // SSD_GROUPS
    for g in range(SSD_GROUPS):
        cg = cm[:, g * SSD_STATE:(g + 1) * SSD_STATE]
        bg = bm[:, g * SSD_STATE:(g + 1) * SSD_STATE]
        scores = _dot_nt(cg, bg)
        cols = slice(g * gw, (g + 1) * gw)
        h_enter = state_ref[:, cols]
        y_inter = _dot(cg, h_enter.astype(BF16)) * dec_in[:, cols]
        state_ref[:, cols] = h_enter * dec_all[:, cols] + _dot_tn(bg, xdec_b[:, cols])
        for pair in range(heads_per_group // 2):
            acc = None
            base = g * gw + pair * LANES
            rhs = xdt_b[:, base:base + LANES]
            for k in range(2):
                h = g * heads_per_group + pair * 2 + k
                seg = cum[:, off_la + h:off_la + h + 1] - cum_t[off_la + h:off_la + h + 1, :]
                m = (scores * jnp.exp(jnp.where(before, seg, NEG_BIG))).astype(BF16)
                keep = low if k == 0 else jnp.logical_not(low)
                part = _dot(m, jnp.where(keep, rhs, jnp.zeros_like(rhs)))
                acc = part if acc is None else acc + part
            y = acc + y_inter[:, pair * LANES:(pair + 1) * LANES]
            if skip is not None:
                y = y + xs[:, base:base + LANES] * skip[:, base:base + LANES]
            y_ref[0, :, base:base + LANES] = y.astype(y_ref.dtype)


def _ssd_scan_kernel(xs_f, b_f, c_f, dtla_f, xs_b, b_b, c_b, dtla_b, skip_ref,
                     yf_ref, yb_ref, st_f, st_b):
    @pl.when(pl.program_id(1) == 0)
    def _():
        st_f[...] = jnp.zeros_like(st_f)
        st_b[...] = jnp.zeros_like(st_b)

    _ssd_chunk(xs_f, b_f, c_f, dtla_f, skip_ref[...], st_f, yf_ref, reverse=False)
    _ssd_chunk(xs_b, b_b, c_b, dtla_b, None, st_b, yb_ref, reverse=True)


def _ssd_scan(xbc, dtla, skip, hp, ctx_chunks):
    n_batch, s_len, _ = xbc.shape
    n_chunks = s_len // CHUNK
    bc_w = SSD_GROUPS * SSD_STATE
    b_blk = hp // bc_w
    fwd = lambda s: s
    bwd = lambda s: _bwd_chunk(s, ctx_chunks, n_chunks)

    def specs(order):
        return [
            pl.BlockSpec((1, CHUNK, hp), lambda b, s: (b, order(s), 0)),
            pl.BlockSpec((1, CHUNK, bc_w), lambda b, s: (b, order(s), b_blk)),
            pl.BlockSpec((1, CHUNK, bc_w), lambda b, s: (b, order(s), b_blk + 1)),
            pl.BlockSpec((1, CHUNK, LANES), lambda b, s: (b, order(s), 0)),
        ]

    return pl.pallas_call(
        _ssd_scan_kernel,
        out_shape=[jax.ShapeDtypeStruct((n_batch, s_len, hp), BF16)] * 2,
        grid=(n_batch, n_chunks),
        in_specs=specs(fwd) + specs(bwd) + [pl.BlockSpec((1, hp), lambda b, s: (0, 0))],
        out_specs=[pl.BlockSpec((1, CHUNK, hp), lambda b, s: (b, fwd(s), 0)),
                   pl.BlockSpec((1, CHUNK, hp), lambda b, s: (b, bwd(s), 0))],
        scratch_shapes=[pltpu.VMEM((SSD_STATE, hp), F32), pltpu.VMEM((SSD_STATE, hp), F32)],
        compiler_params=_cparams(("parallel", "arbitrary")),
    )(xbc, xbc, xbc, dtla, xbc, xbc, xbc, dtla, skip)


def _ret_chunk(q_ref, k_ref, v_ref, lg_ref, state_ref, y_ref, *, reverse):
    qk = q_ref.shape[2] // RET_HEADS
    vd = v_ref.shape[2] // RET_HEADS
    row = lax.broadcasted_iota(jnp.int32, (CHUNK, CHUNK), 0)
    col = lax.broadcasted_iota(jnp.int32, (CHUNK, CHUNK), 1)
    before = (col >= row) if reverse else (col <= row)
    dist = jnp.abs(row - col).astype(F32)
    n_in = ((CHUNK - row) if reverse else (row + 1)).astype(F32)
    n_end = ((row) if reverse else (CHUNK - 1 - row)).astype(F32)
    q = q_ref[0]
    k = k_ref[0]
    v = v_ref[0]
    d_sel = 1 if reverse else 0
    for h in range(RET_HEADS):
        lg = lg_ref[d_sel, h]
        qh = q[:, h * qk:(h + 1) * qk]
        kh = k[:, h * qk:(h + 1) * qk]
        vh = v[:, h * vd:(h + 1) * vd]
        scores = _dot_nt(qh, kh)
        m = (scores * jnp.exp(jnp.where(before, dist * lg, NEG_BIG))).astype(BF16)
        y_intra = _dot(m, vh)
        h_enter = state_ref[h]
        r_in = jnp.exp(n_in * lg)
        y_inter = _dot(qh, h_enter.astype(BF16))
        scale = jnp.concatenate([r_in] * (vd // CHUNK), axis=1)
        y_ref[0, :, h * vd:(h + 1) * vd] = (y_intra + y_inter * scale).astype(y_ref.dtype)
        r_end = jnp.exp(n_end * lg)
        k_scale = jnp.concatenate([r_end] * (qk // CHUNK), axis=1)
        kdec = (kh.astype(F32) * k_scale).astype(BF16)
        state_ref[h] = h_enter * jnp.exp(lg * CHUNK) + _dot_tn(kdec, vh)


def _ret_scan_kernel(lg_ref, q_f, k_f, v_f, q_b, k_b, v_b, yf_ref, yb_ref, st_f, st_b):
    @pl.when(pl.program_id(1) == 0)
    def _():
        st_f[...] = jnp.zeros_like(st_f)
        st_b[...] = jnp.zeros_like(st_b)

    _ret_chunk(q_f, k_f, v_f, lg_ref, st_f, yf_ref, reverse=False)
    _ret_chunk(q_b, k_b, v_b, lg_ref, st_b, yb_ref, reverse=True)


def _ret_scan(q, k, v, log_gamma, ctx_chunks):
    n_batch, s_len, qk_tot = q.shape
    v_tot = v.shape[2]
    n_chunks = s_len // CHUNK
    fwd = lambda s: s
    bwd = lambda s: _bwd_chunk(s, ctx_chunks, n_chunks)

    def specs(order):
        return [
            pl.BlockSpec((1, CHUNK, qk_tot), lambda b, s: (b, order(s), 0)),
            pl.BlockSpec((1, CHUNK, qk_tot), lambda b, s: (b, order(s), 0)),
            pl.BlockSpec((1, CHUNK, v_tot), lambda b, s: (b, order(s), 0)),
        ]

    return pl.pallas_call(
        _ret_scan_kernel,
        out_shape=[jax.ShapeDtypeStruct((n_batch, s_len, v_tot), BF16)] * 2,
        grid=(n_batch, n_chunks),
        in_specs=[pl.BlockSpec(memory_space=pltpu.SMEM)] + specs(fwd) + specs(bwd),
        out_specs=[pl.BlockSpec((1, CHUNK, v_tot), lambda b, s: (b, fwd(s), 0)),
                   pl.BlockSpec((1, CHUNK, v_tot), lambda b, s: (b, bwd(s), 0))],
        scratch_shapes=[pltpu.VMEM((RET_HEADS, qk_tot // RET_HEADS, v_tot // RET_HEADS), F32)] * 2,
        compiler_params=_cparams(("parallel", "arbitrary")),
    )(log_gamma, q, k, v, q, k, v)


def _readout_kernel(yf_ref, yb_ref, gate_ref, ng_ref, w_ref, x_ref, mod_ref, out_ref, *,
                    gate_first, group, gate_idx):
    y = yf_ref[0].astype(F32) + yb_ref[0].astype(F32)
    gate = _silu(gate_ref[0].astype(F32))
    if gate_first:
        y = y * gate
    pieces = []
    for j in range(y.shape[1] // group):
        yg = y[:, j * group:(j + 1) * group]
        ms = jnp.mean(yg * yg, axis=-1, keepdims=True)
        pieces.append(yg * lax.rsqrt(ms + NORM_EPS))
    y = jnp.concatenate(pieces, axis=1)
    y = y * ng_ref[...] if gate_first else y * gate
    o = _dot(y.astype(BF16), w_ref[...])
    out_ref[0] = x_ref[0] + mod_ref[0][gate_idx:gate_idx + 1] * o


def _readout(y_f, y_b, gate, norm_g, w_out, x, mod, ctx_tiles, gate_first, group):
    n_batch, s_len, inner = y_f.shape
    d = x.shape[2]
    n_tiles = s_len // ROW_TILE
    row = lambda b, t: (b, t, 0)
    return pl.pallas_call(
        functools.partial(_readout_kernel, gate_first=gate_first, group=group, gate_idx=2),
        out_shape=jax.ShapeDtypeStruct(x.shape, F32),
        grid=(n_batch, n_tiles),
        in_specs=[
            pl.BlockSpec((1, ROW_TILE, inner), row),
            pl.BlockSpec((1, ROW_TILE, inner), row),
            pl.BlockSpec((1, ROW_TILE, inner), row),
            pl.BlockSpec((1, inner), lambda b, t: (0, 0)),
            pl.BlockSpec((inner, d), lambda b, t: (0, 0)),
            pl.BlockSpec((1, ROW_TILE, d), row),
            pl.BlockSpec((1, 6, d), lambda b, t: (jnp.where(t < ctx_tiles, n_batch, b), 0, 0)),
        ],
        out_specs=pl.BlockSpec((1, ROW_TILE, d), row),
        compiler_params=_cparams(("parallel", "arbitrary")),
    )(y_f, y_b, gate, norm_g.reshape(1, inner), w_out, x, mod)


def _router_kernel(x_ref, g_ref, mod_ref, wr_ref, br_ref, h_ref, idx_ref, gate_ref, rank_ref,
                   count_ref, base_ref):
    @pl.when(jnp.logical_and(pl.program_id(0) == 0, pl.program_id(1) == 0))
    def _():
        base_ref[...] = jnp.zeros_like(base_ref)

    h = _norm_modulate(x_ref[0], g_ref[...], mod_ref[0], 3, 4)
    h_ref[0] = h
    h_hi, h_lo = _split_bf16(h)
    w_hi, w_lo = _split_bf16(wr_ref[...])
    logits = _dot(h_hi, w_hi) + _dot(h_lo, w_hi) + _dot(h_hi, w_lo) + br_ref[...]
    lane = lax.broadcasted_iota(jnp.int32, logits.shape, 1).astype(F32)
    work = logits
    idx_out = jnp.zeros(logits.shape, F32)
    val_out = jnp.full(logits.shape, NEG_BIG, F32)
    picks = []
    for k in range(TOP_K):
        m = jnp.max(work, axis=-1, keepdims=True)
        first = jnp.min(jnp.where(work == m, lane, float(LANES)), axis=-1, keepdims=True)
        pick = lane == first
        picks.append(pick)
        idx_out = jnp.where(lane == k, first, idx_out)
        val_out = jnp.where(lane == k, m, val_out)
        work = jnp.where(pick, -jnp.inf, work)
    top = jnp.max(val_out, axis=-1, keepdims=True)
    e = jnp.exp(val_out - top)
    idx_ref[0] = idx_out.astype(jnp.int32)
    gate_ref[0] = e / jnp.sum(e, axis=-1, keepdims=True)

    rows = logits.shape[0]
    onehot = jnp.zeros(logits.shape, F32)
    for pick in picks:
        onehot = onehot + jnp.where(pick, 1.0, 0.0)
    r = lax.broadcasted_iota(jnp.int32, (rows, rows), 0)
    c = lax.broadcasted_iota(jnp.int32, (rows, rows), 1)
    strict = jnp.where(c < r, 1.0, 0.0).astype(BF16)
    earlier = _dot(strict, onehot.astype(BF16)) + base_ref[...]
    rank_out = jnp.zeros(logits.shape, F32)
    for k, pick in enumerate(picks):
        rk = jnp.sum(jnp.where(pick, earlier, 0.0), axis=-1, keepdims=True)
        rank_out = jnp.where(lane == k, rk, rank_out)
    rank_ref[0] = rank_out.astype(jnp.int32)
    base_ref[...] = base_ref[...] + jnp.sum(onehot, axis=0, keepdims=True)
    count_ref[...] = base_ref[...].astype(jnp.int32)


def _router(x, g, mod, w_router, b_router, ctx_tiles):
    n_batch, s_len, d = x.shape
    n_exp = w_router.shape[1]
    wr = jnp.zeros((d, LANES), F32).at[:, :n_exp].set(w_router)
    br = jnp.full((1, LANES), NEG_BIG, F32).at[0, :n_exp].set(b_router)
    row = lambda b, t: (b, t, 0)
    return pl.pallas_call(
        _router_kernel,
        out_shape=[jax.ShapeDtypeStruct((n_batch, s_len, d), F32),
                   jax.ShapeDtypeStruct((n_batch, s_len, LANES), jnp.int32),
                   jax.ShapeDtypeStruct((n_batch, s_len, LANES), F32),
                   jax.ShapeDtypeStruct((n_batch, s_len, LANES), jnp.int32),
                   jax.ShapeDtypeStruct((1, LANES), jnp.int32)],
        grid=(n_batch, s_len // ROW_TILE),
        in_specs=[
            pl.BlockSpec((1, ROW_TILE, d), row),
            pl.BlockSpec((1, d), lambda b, t: (0, 0)),
            pl.BlockSpec((1, 6, d), lambda b, t: (jnp.where(t < ctx_tiles, n_batch, b), 0, 0)),
            pl.BlockSpec((d, LANES), lambda b, t: (0, 0)),
            pl.BlockSpec((1, LANES), lambda b, t: (0, 0)),
        ],
        out_specs=[pl.BlockSpec((1, ROW_TILE, d), row),
                   pl.BlockSpec((1, ROW_TILE, LANES), row),
                   pl.BlockSpec((1, ROW_TILE, LANES), row),
                   pl.BlockSpec((1, ROW_TILE, LANES), row),
                   pl.BlockSpec((1, LANES), lambda b, t: (0, 0))],
        scratch_shapes=[pltpu.VMEM((1, LANES), F32)],
        compiler_params=_cparams(("arbitrary", "arbitrary")),
    )(x, g.reshape(1, d), mod, wr, br)


def _on_slot(parity_of, fn):
    for s in range(2):
        @pl.when(parity_of % 2 == s)
        def _(s=s):
            fn(s)


def _indexed_row_pipeline(i, n, idx_hbm, idx_smem, isem, issue_rows):
    def idx_copy(tile, slot):
        return pltpu.make_async_copy(idx_hbm.at[tile], idx_smem[slot], isem.at[slot])

    @pl.when(i == 0)
    def _():
        idx_copy(0, 0).start()
        idx_copy(0, 0).wait()
        issue_rows(0, 0)

        @pl.when(n > 1)
        def _():
            idx_copy(1, 1).start()

    @pl.when(i + 1 < n)
    def _():
        def nxt(s):
            idx_copy(i + 1, s).wait()
            issue_rows(i + 1, s)
        _on_slot(i + 1, nxt)

    @pl.when(i + 2 < n)
    def _():
        _on_slot(i + 2, lambda s: idx_copy(i + 2, s).start())


def _dispatch_kernel(zblk_ref, nused_ref, pos_hbm, h_hbm, xs_hbm, zero_buf, idx_a, idx_b, zsem, gsem,
                     isem):
    i = pl.program_id(0)
    n = pl.num_programs(0)
    n_rows = TOP_K * ROW_TILE

    @pl.when(i == 0)
    def _():
        zero_buf[...] = jnp.zeros_like(zero_buf)

        def zero_block(blk):
            cp = pltpu.make_async_copy(
                zero_buf, xs_hbm.at[pl.ds(blk * EXPERT_ROWS, EXPERT_ROWS), :], zsem)
            cp.start()
            cp.wait()

        for e in range(zblk_ref.shape[0]):
            @pl.when(zblk_ref[e] >= 0)
            def _(e=e):
                zero_block(zblk_ref[e])

        def tail(blk, carry):
            zero_block(blk)
            return carry
        lax.fori_loop(nused_ref[0], xs_hbm.shape[0] // EXPERT_ROWS, tail, 0)

    def issue_rows(tile, slot):
        idx = (idx_a, idx_b)[slot]

        def body(r, carry):
            tok = tile * ROW_TILE + r // TOP_K
            pltpu.make_async_copy(h_hbm.at[pl.ds(tok, 1), :], xs_hbm.at[pl.ds(idx[r], 1), :],
                                  gsem.at[slot]).start()
            return carry
        lax.fori_loop(0, n_rows, body, 0, unroll=DMA_UNROLL)

    def wait_rows(slot):
        pltpu.make_async_copy(h_hbm.at[pl.ds(0, n_rows), :], xs_hbm.at[pl.ds(0, n_rows), :],
                              gsem.at[slot]).wait()

    _indexed_row_pipeline(i, n, pos_hbm, (idx_a, idx_b), isem, issue_rows)
    _on_slot(i, wait_rows)


def _dispatch(pos_tiles, zero_blocks, n_used, h_flat, cap):
    n_tiles = pos_tiles.shape[0]
    d = h_flat.shape[1]
    grid_spec = pltpu.PrefetchScalarGridSpec(
        num_scalar_prefetch=2,
        grid=(n_tiles,),
        in_specs=[pl.BlockSpec(memory_space=pl.ANY), pl.BlockSpec(memory_space=pl.ANY)],
        out_specs=pl.BlockSpec(memory_space=pl.ANY),
        scratch_shapes=[
            pltpu.VMEM((EXPERT_ROWS, d), F32),
            pltpu.SMEM((TOP_K * ROW_TILE,), jnp.int32),
            pltpu.SMEM((TOP_K * ROW_TILE,), jnp.int32),
            pltpu.SemaphoreType.DMA,
            pltpu.SemaphoreType.DMA((2,)),
            pltpu.SemaphoreType.DMA((2,)),
        ],
    )
    return pl.pallas_call(
        _dispatch_kernel,
        out_shape=jax.ShapeDtypeStruct((cap, d), F32),
        grid_spec=grid_spec,
        compiler_params=_cparams(("arbitrary",)),
    )(zero_blocks, n_used, pos_tiles, h_flat)


def _expert_kernel(bexp_ref, nused_ref, x_ref, w1_ref, b1_ref, w2_ref, b2_ref, out_ref, w1p, w2p):
    i = pl.program_id(0)
    prev = bexp_ref[jnp.maximum(i - 1, 0)]
    f2 = w1_ref.shape[2]

    @pl.when(jnp.logical_or(i == 0, bexp_ref[i] != prev))
    def _():
        src = lax.broadcasted_iota(jnp.int32, (MXU_DIM, MXU_DIM), 0)
        dst = lax.broadcasted_iota(jnp.int32, (MXU_DIM, MXU_DIM), 1)
        want = jnp.where(dst < LANES, 2 * dst, 2 * (dst - LANES) + 1)
        perm = jnp.where(src == want, 1.0, 0.0).astype(BF16)
        for c in range(f2 // MXU_DIM):
            blk = w1_ref[0, :, c * MXU_DIM:(c + 1) * MXU_DIM].astype(BF16)
            w1p[:, c * MXU_DIM:(c + 1) * MXU_DIM] = _dot(blk, perm).astype(BF16)
        w2p[...] = w2_ref[0].astype(BF16)

    @pl.when(i < nused_ref[0])
    def _():
        x = x_ref[...].astype(BF16)
        hdn = _dot(x, w1p[...]) + b1_ref[0]
        acts = []
        for c in range(f2 // MXU_DIM):
            glu = jnp.minimum(hdn[:, c * MXU_DIM:c * MXU_DIM + LANES], SWIGLU_LIMIT)
            lin = jnp.clip(hdn[:, c * MXU_DIM + LANES:(c + 1) * MXU_DIM], -SWIGLU_LIMIT, SWIGLU_LIMIT)
            acts.append(glu * _sigmoid(SWIGLU_ALPHA * glu) * (lin + 1.0))
        act = jnp.concatenate(acts, axis=1).astype(BF16)
        out_ref[...] = _dot(act, w2p[...]) + b2_ref[0]

    @pl.when(i >= nused_ref[0])
    def _():
        out_ref[...] = jnp.zeros_like(out_ref)


def _experts(x_sorted, block_expert, n_used, w1, b1p, w2, b2):
    n_blocks = block_expert.shape[0]
    d = x_sorted.shape[1]
    n_exp, _, f2 = w1.shape
    f = w2.shape[1]
    wspec = lambda shape: pl.BlockSpec(shape, lambda i, be, nu: (be[i], 0, 0))
    grid_spec = pltpu.PrefetchScalarGridSpec(
        num_scalar_prefetch=2,
        grid=(n_blocks,),
        in_specs=[
            pl.BlockSpec((EXPERT_ROWS, d), lambda i, be, nu: (jnp.minimum(i, nu[0] - 1), 0)),
            wspec((1, d, f2)), wspec((1, 1, f2)), wspec((1, f, d)), wspec((1, 1, d)),
        ],
        out_specs=pl.BlockSpec((EXPERT_ROWS, d), lambda i, be, nu: (i, 0)),
        scratch_shapes=[pltpu.VMEM((d, f2), BF16), pltpu.VMEM((f, d), BF16)],
    )
    return pl.pallas_call(
        _expert_kernel,
        out_shape=jax.ShapeDtypeStruct((n_blocks * EXPERT_ROWS, d), F32),
        grid_spec=grid_spec,
        compiler_params=_cparams(("arbitrary",)),
    )(block_expert, n_used, x_sorted, w1, b1p, w2, b2.reshape(n_exp, 1, d))


def _combine_kernel(pos_hbm, y_hbm, gate_ref, x_ref, mod_ref, out_ref, ybuf, idx_a, idx_b, gsem, isem):
    i = pl.program_id(0)
    n = pl.num_programs(0)
    n_rows = TOP_K * ROW_TILE

    def issue_rows(tile, slot):
        idx = (idx_a, idx_b)[slot]

        def body(r, carry):
            pltpu.make_async_copy(y_hbm.at[pl.ds(idx[r], 1), :], ybuf.at[slot, pl.ds(r, 1), :],
                                  gsem.at[slot]).start()
            return carry
        lax.fori_loop(0, n_rows, body, 0, unroll=DMA_UNROLL)

    def wait_rows(slot):
        pltpu.make_async_copy(y_hbm.at[pl.ds(0, n_rows), :], ybuf.at[slot], gsem.at[slot]).wait()

    _indexed_row_pipeline(i, n, pos_hbm, (idx_a, idx_b), isem, issue_rows)
    _on_slot(i, wait_rows)

    def compute(s):
        gate = gate_ref[...]
        acc = None
        for k in range(TOP_K):
            part = ybuf[s, k * ROW_TILE:(k + 1) * ROW_TILE, :] * gate[:, k:k + 1]
            acc = part if acc is None else acc + part
        out_ref[...] = x_ref[...] + mod_ref[0][5:6] * acc
    _on_slot(i, compute)


def _combine(pos_tiles, y_rows, gate_flat, x_flat, mod, n_batch, tiles_per_batch, ctx_tiles):
    n_tiles = pos_tiles.shape[0]
    d = x_flat.shape[1]

    def mod_map(i):
        b = i // tiles_per_batch
        return (jnp.where(i % tiles_per_batch < ctx_tiles, n_batch, b), 0, 0)

    return pl.pallas_call(
        _combine_kernel,
        out_shape=jax.ShapeDtypeStruct(x_flat.shape, F32),
        grid=(n_tiles,),
        in_specs=[
            pl.BlockSpec(memory_space=pl.ANY),
            pl.BlockSpec(memory_space=pl.ANY),
            pl.BlockSpec((ROW_TILE, LANES), lambda i: (i, 0)),
            pl.BlockSpec((ROW_TILE, d), lambda i: (i, 0)),
            pl.BlockSpec((1, 6, d), mod_map),
        ],
        out_specs=pl.BlockSpec((ROW_TILE, d), lambda i: (i, 0)),
        scratch_shapes=[
            pltpu.VMEM((2, TOP_K * ROW_TILE, d), F32),
            pltpu.SMEM((TOP_K * ROW_TILE,), jnp.int32),
            pltpu.SMEM((TOP_K * ROW_TILE,), jnp.int32),
            pltpu.SemaphoreType.DMA((2,)),
            pltpu.SemaphoreType.DMA((2,)),
        ],
        compiler_params=_cparams(("arbitrary",)),
    )(pos_tiles, y_rows, gate_flat, x_flat, mod)


def _moe(x, g, mod, w_router, b_router, w1, b1p, w2, b2, ctx_tiles):
    n_batch, s_len, d = x.shape
    n_tok = n_batch * s_len
    n_exp = w1.shape[0]
    n_tiles = n_tok // ROW_TILE
    h, idx, gate, rank, count = _router(x, g, mod, w_router, b_router, ctx_tiles)

    counts = count[0, :n_exp]
    padded = (counts + EXPERT_ROWS - 1) // EXPERT_ROWS * EXPERT_ROWS
    pend = jnp.cumsum(padded)
    pstart = pend - padded
    n_blocks = -(-(n_tok * TOP_K) // EXPERT_ROWS) + n_exp
    block_start = jnp.arange(n_blocks, dtype=jnp.int32) * EXPERT_ROWS
    block_expert = jnp.minimum(jnp.sum(block_start[:, None] >= pend[None, :], axis=1),
                               n_exp - 1).astype(jnp.int32)
    n_used = (pend[-1] // EXPERT_ROWS).astype(jnp.int32).reshape(1)
    zero_blocks = jnp.where(padded > counts, pend // EXPERT_ROWS - 1, -1).astype(jnp.int32)

    top_i = idx[..., :TOP_K].reshape(n_tok, TOP_K)
    pos = (pstart[top_i] + rank[..., :TOP_K].reshape(n_tok, TOP_K)).astype(jnp.int32)
    pos_token_major = pos.reshape(n_tiles, ROW_TILE * TOP_K)
    pos_slot_major = pos.reshape(n_tiles, ROW_TILE, TOP_K).transpose(0, 2, 1).reshape(n_tiles, -1)

    x_sorted = _dispatch(pos_token_major, zero_blocks, n_used, h.reshape(n_tok, d),
                         n_blocks * EXPERT_ROWS)
    y_rows = _experts(x_sorted, block_expert, n_used, w1, b1p, w2, b2)
    out = _combine(pos_slot_major, y_rows, gate.reshape(n_tok, LANES), x.reshape(n_tok, d), mod,
                   n_batch, s_len // ROW_TILE, ctx_tiles)
    return out.reshape(n_batch, s_len, d)


def _final_norm_kernel(x_ref, g_ref, out_ref):
    x = x_ref[0]
    ms = jnp.mean(x * x, axis=-1, keepdims=True)
    out_ref[0] = x * lax.rsqrt(ms + NORM_EPS) * g_ref[...]


def _final_norm(x, g, ctx_tiles, seq_len):
    n_batch, _, d = x.shape
    return pl.pallas_call(
        _final_norm_kernel,
        out_shape=jax.ShapeDtypeStruct((n_batch, seq_len, d), F32),
        grid=(n_batch, seq_len // ROW_TILE),
        in_specs=[pl.BlockSpec((1, ROW_TILE, d), lambda b, t: (b, t + ctx_tiles, 0)),
                  pl.BlockSpec((1, d), lambda b, t: (0, 0))],
        out_specs=pl.BlockSpec((1, ROW_TILE, d), lambda b, t: (b, t, 0)),
        compiler_params=_cparams(("parallel", "arbitrary")),
    )(x, g.reshape(1, d))


def _rope_tables(seq_len, ctx_len, half_dim):
    rows = seq_len // GRID_W
    t = jnp.arange(seq_len)
    row_pos = (t // GRID_W).astype(F32) - (rows - 1) / 2.0
    col_pos = (t % GRID_W).astype(F32) - (GRID_W - 1) / 2.0
    inv_freq = ROPE_BASE ** (-jnp.arange(0, half_dim, 2, dtype=F32) / half_dim)

    def tables(pos):
        ang = pos[:, None] * inv_freq[None, :]
        c, s = jnp.cos(ang), jnp.sin(ang)
        return jnp.concatenate([c, c], axis=1), jnp.concatenate([-s, s], axis=1)

    cr, sr = tables(row_pos)
    cc, sc = tables(col_pos)
    cos = jnp.concatenate([cr, cc], axis=1)
    sin = jnp.concatenate([sr, sc], axis=1)
    cos = jnp.concatenate([jnp.ones((ctx_len, cos.shape[1]), F32), cos], axis=0)
    sin = jnp.concatenate([jnp.zeros((ctx_len, sin.shape[1]), F32), sin], axis=0)
    return cos, sin


def kernel(x, c, ctx, c_ctx, w_mod, b_mod, norm1_g, norm2_g, ssd_w_in, ssd_conv_w, ssd_conv_b,
           ssd_dt_bias_f, ssd_dt_bias_b, ssd_a_log_f, ssd_a_log_b, ssd_d, ssd_norm_g, ssd_w_out,
           ret_w_in, ret_scale_f, ret_scale_b, ret_w_out, router_w, router_b,
           moe_w1, moe_b1, moe_w2, moe_b2, final_g):
    n_batch, seq_len, d = x.shape
    ctx_len = ctx.shape[1]
    depth = w_mod.shape[0]
    assert ctx_len % ROW_TILE == 0 and seq_len % ROW_TILE == 0
    ctx_tiles = ctx_len // ROW_TILE
    ctx_chunks = ctx_len // CHUNK

    n_heads = ssd_d.shape[1]
    ssd_inner = n_heads * SSD_HEAD_DIM
    bc_w = SSD_GROUPS * SSD_STATE
    conv_dim = ssd_inner + 2 * bc_w
    assert n_heads == LANES // 4 and ssd_inner % bc_w == 0

    ret_qk = (ret_w_in.shape[2] - 2 * ret_w_out.shape[1]) // 2
    ret_v = ret_w_out.shape[1]
    qk_dim = ret_qk // RET_HEADS
    assert qk_dim == 2 * LANES

    n_exp, _, f2 = moe_w1.shape[1:]
    assert f2 % MXU_DIM == 0

    cond = jnp.concatenate([c, c_ctx[None, :], jnp.zeros((16 - n_batch - 1, d), F32)], axis=0)
    mod_all = _modulation(cond, w_mod, b_mod).reshape(depth, 16, 6, d)

    xs = jnp.concatenate([ctx, x], axis=1)
    cos, sin = _rope_tables(seq_len, ctx_len, qk_dim // 2)

    for i in range(depth):
        j = i // 2
        mod = mod_all[i]
        if i % 2 == 0:
            w_in = ssd_w_in[j]
            w_main = w_in[:, :ssd_inner + conv_dim].astype(BF16)
            w_dt = w_in[:, ssd_inner + conv_dim:].astype(BF16)
            w_cat = jnp.concatenate([w_main, w_dt, w_dt], axis=1)
            splits = [(0, ssd_inner), (ssd_inner, conv_dim), (ssd_inner + conv_dim, LANES)]
            z, xbc, dt_raw = _in_proj(xs, norm1_g[i], mod, w_cat, splits, [BF16, BF16, F32], ctx_tiles)
            dt_bias2 = jnp.concatenate([ssd_dt_bias_f[j], ssd_dt_bias_b[j]])
            dt_bias4 = jnp.concatenate([dt_bias2, dt_bias2]).reshape(1, LANES)
            a_log4 = jnp.concatenate([jnp.zeros((2 * n_heads,), F32), ssd_a_log_f[j],
                                      ssd_a_log_b[j]]).reshape(1, LANES)
            xbc_c, dtla = _ssd_conv(xbc, dt_raw, ssd_conv_w[j], ssd_conv_b[j], dt_bias4, a_log4,
                                    ctx_chunks)
            skip = jnp.repeat(ssd_d[j], SSD_HEAD_DIM).reshape(1, ssd_inner)
            y_f, y_b = _ssd_scan(xbc_c, dtla, skip, ssd_inner, ctx_chunks)
            xs = _readout(y_f, y_b, z, ssd_norm_g[j], ssd_w_out[j].astype(BF16), xs, mod,
                          ctx_tiles, True, ssd_inner // SSD_GROUPS)
        else:
            w_cat = ret_w_in[j].astype(BF16)
            splits = [(0, ret_qk), (ret_qk, ret_qk), (2 * ret_qk, ret_v), (2 * ret_qk + ret_v, ret_v)]
            q, k, v, gt = _in_proj(xs, norm1_g[i], mod, w_cat, splits, [BF16] * 4, ctx_tiles,
                                   rope=(cos, sin, (0, 1)), k_scale=qk_dim ** -0.5)
            log_gamma = jnp.stack([jnp.log1p(-jnp.exp2(-ret_scale_f[j].astype(F32))),
                                   jnp.log1p(-jnp.exp2(-ret_scale_b[j].astype(F32)))])
            y_f, y_b = _ret_scan(q, k, v, log_gamma, ctx_chunks)
            xs = _readout(y_f, y_b, gt, jnp.ones((ret_v,), F32), ret_w_out[j].astype(BF16), xs, mod,
                          ctx_tiles, False, ret_v // RET_HEADS)
        b1p = moe_b1[i].reshape(n_exp, f2 // MXU_DIM, LANES, 2).transpose(0, 1, 3, 2).reshape(n_exp, 1, f2)
        xs = _moe(xs, norm2_g[i], mod, router_w[i], router_b[i], moe_w1[i], b1p, moe_w2[i], moe_b2[i],
                  ctx_tiles)
    return _final_norm(xs, final_g, ctx_tiles, seq_len)
```

```python
import functools

import jax
import jax.numpy as jnp
from jax import lax
from jax.experimental import pallas as pl
from jax.experimental.pallas import tpu as pltpu

F32 = jnp.float32
BF16 = jnp.bfloat16

NORM_EPS = 1e-6
CHUNK = 128
GRID_W = 64
ROPE_BASE = 10000.0

SSD_HEAD_DIM = 64
SSD_GROUPS = 4
SSD_STATE = 128
SSD_CONV = 5

RET_HEADS = 4

TOP_K = 4
SWIGLU_ALPHA = 1.702
SWIGLU_LIMIT = 7.0

LANES = 128
MXU_DIM = 256
ROW_TILE = 256
EXPERT_ROWS = 256
HALO = 16
DMA_UNROLL = 8
VMEM_LIMIT = 56 * 1024 * 1024
NEG_BIG = -1e30


def _cparams(sem):
    return pltpu.CompilerParams(dimension_semantics=sem, vmem_limit_bytes=VMEM_LIMIT)


def _split_bf16(a):
    hi = a.astype(BF16)
    lo = (a - hi.astype(F32)).astype(BF16)
    return hi, lo


def _dot(a, b):
    return jnp.dot(a, b, preferred_element_type=F32)


def _dot_tn(a, b):
    return lax.dot_general(a, b, (((0,), (0,)), ((), ())), preferred_element_type=F32)


def _dot_nt(a, b):
    return lax.dot_general(a, b, (((1,), (1,)), ((), ())), preferred_element_type=F32)


def _dot_split_lhs(a, b_bf16):
    hi, lo = _split_bf16(a)
    return _dot(hi, b_bf16) + _dot(lo, b_bf16)


def _sigmoid(x):
    return 1.0 / (1.0 + jnp.exp(-x))


def _silu(x):
    return x * _sigmoid(x)


def _mod_kernel(cond_ref, w_ref, b_ref, out_ref):
    s = _silu(cond_ref[...])
    s_hi, s_lo = _split_bf16(s)
    w_hi, w_lo = _split_bf16(w_ref[0])
    acc = _dot(s_hi, w_hi) + _dot(s_lo, w_hi) + _dot(s_hi, w_lo)
    out_ref[0] = acc + b_ref[0]


def _modulation(cond, w_mod, b_mod):
    depth, d, n = w_mod.shape
    rows = cond.shape[0]
    tn = 1024
    return pl.pallas_call(
        _mod_kernel,
        out_shape=jax.ShapeDtypeStruct((depth, rows, n), F32),
        grid=(depth, n // tn),
        in_specs=[
            pl.BlockSpec((rows, d), lambda l, j: (0, 0)),
            pl.BlockSpec((1, d, tn), lambda l, j: (l, 0, j)),
            pl.BlockSpec((1, 1, tn), lambda l, j: (l, 0, j)),
        ],
        out_specs=pl.BlockSpec((1, rows, tn), lambda l, j: (l, 0, j)),
        compiler_params=_cparams(("arbitrary", "arbitrary")),
    )(cond, w_mod, b_mod.reshape(depth, 1, n))


def _norm_modulate(x, g, mod, shift_idx, scale_idx):
    ms = jnp.mean(x * x, axis=-1, keepdims=True)
    h = x * lax.rsqrt(ms + NORM_EPS) * g
    return h * (1.0 + mod[scale_idx:scale_idx + 1]) + mod[shift_idx:shift_idx + 1]


def _in_proj_kernel(*refs, splits, rope_outs, k_scale):
    if rope_outs:
        x_ref, g_ref, mod_ref, w_ref, cos_ref, sin_ref = refs[:6]
        out_refs = refs[6:]
    else:
        x_ref, g_ref, mod_ref, w_ref = refs[:4]
        out_refs = refs[4:]
    h = _norm_modulate(x_ref[0], g_ref[...], mod_ref[0], 0, 1).astype(BF16)
    for idx, (start, width) in enumerate(splits):
        o = _dot(h, w_ref[:, start:start + width])
        if idx in rope_outs:
            cos = cos_ref[...]
            sin = sin_ref[...]
            pieces = []
            for j in range(width // LANES):
                u = o[:, j * LANES:(j + 1) * LANES]
                half = (j % 2) * LANES
                r = (u * cos[:, half:half + LANES]
                     + pltpu.roll(u, LANES // 2, axis=1) * sin[:, half:half + LANES])
                pieces.append(r)
            o = jnp.concatenate(pieces, axis=1)
            if idx == rope_outs[-1]:
                o = o * k_scale
        out_refs[idx][0] = o.astype(out_refs[idx].dtype)


def _in_proj(x, g, mod, w, splits, out_dtypes, ctx_tiles, rope=None, k_scale=1.0):
    n_batch, s_len, d = x.shape
    n_tiles = s_len // ROW_TILE
    in_specs = [
        pl.BlockSpec((1, ROW_TILE, d), lambda b, t: (b, t, 0)),
        pl.BlockSpec((1, d), lambda b, t: (0, 0)),
        pl.BlockSpec((1, 6, d), lambda b, t: (jnp.where(t < ctx_tiles, n_batch, b), 0, 0)),
        pl.BlockSpec(w.shape, lambda b, t: (0, 0)),
    ]
    args = [x, g.reshape(1, d), mod, w]
    rope_outs = ()
    if rope is not None:
        cos, sin, rope_outs = rope
        in_specs += [pl.BlockSpec((ROW_TILE, cos.shape[1]), lambda b, t: (t, 0)),
                     pl.BlockSpec((ROW_TILE, sin.shape[1]), lambda b, t: (t, 0))]
        args += [cos, sin]
    out_shape = [jax.ShapeDtypeStruct((n_batch, s_len, wd), dt)
                 for (_, wd), dt in zip(splits, out_dtypes)]
    out_specs = [pl.BlockSpec((1, ROW_TILE, wd), lambda b, t: (b, t, 0)) for _, wd in splits]
    return pl.pallas_call(
        functools.partial(_in_proj_kernel, splits=tuple(splits), rope_outs=tuple(rope_outs),
                          k_scale=k_scale),
        out_shape=out_shape,
        grid=(n_batch, n_tiles),
        in_specs=in_specs,
        out_specs=out_specs,
        compiler_params=_cparams(("parallel", "arbitrary")),
    )(*args)


def _ssd_conv_kernel(cur_ref, prev_ref, next_ref, dt_ref, cw_ref, cb_ref, dtb_ref, alog_ref,
                     xbc_ref, dtla_ref, ext_ref, *, ctx_chunks, n_chunks):
    c = pl.program_id(1)
    has_prev = jnp.logical_and(c != 0, c != ctx_chunks)
    has_next = jnp.logical_and(c != ctx_chunks - 1, c != n_chunks - 1)
    pad = SSD_CONV // 2
    ext_ref[0:8, :] = prev_ref[0].astype(F32)[HALO - 8:HALO, :] * jnp.where(has_prev, 1.0, 0.0)
    ext_ref[8:8 + CHUNK, :] = cur_ref[0].astype(F32)
    ext_ref[8 + CHUNK:16 + CHUNK, :] = next_ref[0].astype(F32)[0:8, :] * jnp.where(has_next, 1.0, 0.0)
    cw = cw_ref[...]
    acc = jnp.zeros((CHUNK, cur_ref.shape[2]), F32) + cb_ref[...]
    for k in range(SSD_CONV):
        acc = acc + ext_ref[8 - pad + k:8 - pad + k + CHUNK, :] * cw[k:k + 1, :]
    xbc_ref[0] = _silu(acc).astype(xbc_ref.dtype)
    raw = dt_ref[0] + dtb_ref[...]
    dt = jnp.maximum(raw, 0.0) + jnp.log(1.0 + jnp.exp(-jnp.abs(raw)))
    lane = lax.broadcasted_iota(jnp.int32, dt.shape, 1)
    a = -jnp.exp(alog_ref[...])
    dtla_ref[0] = jnp.where(lane < LANES // 2, dt, dt * a)


def _ssd_conv(xbc, dt_raw, conv_w, conv_b, dt_bias4, a_log4, ctx_chunks):
    n_batch, s_len, cdim = xbc.shape
    n_chunks = s_len // CHUNK
    per = CHUNK // HALO
    last_halo = s_len // HALO - 1
    return pl.pallas_call(
        functools.partial(_ssd_conv_kernel, ctx_chunks=ctx_chunks, n_chunks=n_chunks),
        out_shape=[jax.ShapeDtypeStruct((n_batch, s_len, cdim), BF16),
                   jax.ShapeDtypeStruct((n_batch, s_len, LANES), F32)],
        grid=(n_batch, n_chunks),
        in_specs=[
            pl.BlockSpec((1, CHUNK, cdim), lambda b, c: (b, c, 0)),
            pl.BlockSpec((1, HALO, cdim), lambda b, c: (b, jnp.maximum(c * per - 1, 0), 0)),
            pl.BlockSpec((1, HALO, cdim), lambda b, c: (b, jnp.minimum(c * per + per, last_halo), 0)),
            pl.BlockSpec((1, CHUNK, LANES), lambda b, c: (b, c, 0)),
            pl.BlockSpec((SSD_CONV, cdim), lambda b, c: (0, 0)),
            pl.BlockSpec((1, cdim), lambda b, c: (0, 0)),
            pl.BlockSpec((1, LANES), lambda b, c: (0, 0)),
            pl.BlockSpec((1, LANES), lambda b, c: (0, 0)),
        ],
        out_specs=[pl.BlockSpec((1, CHUNK, cdim), lambda b, c: (b, c, 0)),
                   pl.BlockSpec((1, CHUNK, LANES), lambda b, c: (b, c, 0))],
        scratch_shapes=[pltpu.VMEM((CHUNK + 16, cdim), F32)],
        compiler_params=_cparams(("parallel", "arbitrary")),
    )(xbc, xbc, xbc, dt_raw, conv_w, conv_b.reshape(1, cdim), dt_bias4, a_log4)


def _bwd_chunk(s, ctx_chunks, n_chunks):
    return jnp.where(s < ctx_chunks, ctx_chunks - 1 - s, n_chunks - 1 - (s - ctx_chunks))


def _ssd_chunk(xs_ref, b_ref, c_ref, dtla_ref, skip, state_ref, y_ref, *, reverse):
    n_heads = xs_ref.shape[2] // SSD_HEAD_DIM
    hp = xs_ref.shape[2]
    gw = hp // SSD_GROUPS
    off_dt = LANES // 4 if reverse else 0
    off_la = off_dt + LANES // 2

    row = lax.broadcasted_iota(jnp.int32, (CHUNK, CHUNK), 0)
    col = lax.broadcasted_iota(jnp.int32, (CHUNK, CHUNK), 1)
    before = (col >= row) if reverse else (col <= row)
    tri = jnp.where(before, 1.0, 0.0).astype(BF16)
    tri_t = jnp.where((row >= col) if reverse else (row <= col), 1.0, 0.0).astype(BF16)

    dtla = dtla_ref[0]
    hi, lo = _split_bf16(dtla)
    cum = _dot(tri, hi) + _dot(tri, lo)
    cum_t = _dot_tn(hi, tri_t) + _dot_tn(lo, tri_t)
    cum = jnp.where(lax.broadcasted_iota(jnp.int32, cum.shape, 1) >= LANES // 2, cum, 0.0)
    last = 0 if reverse else CHUNK - 1
    total = cum[last:last + 1, :]

    sec = lax.broadcasted_iota(jnp.int32, (LANES, hp), 0)
    head = lax.broadcasted_iota(jnp.int32, (LANES, hp), 1) // SSD_HEAD_DIM
    e_dt = jnp.where(sec == head + off_dt, 1.0, 0.0).astype(BF16)
    e_la = jnp.where(sec == head + off_la, 1.0, 0.0).astype(BF16)

    xs = xs_ref[0].astype(F32)
    xdt = xs * _dot_split_lhs(dtla, e_dt)
    xdt_b = xdt.astype(BF16)
    dec_in = _dot_split_lhs(jnp.exp(cum), e_la)
    dec_end = _dot_split_lhs(jnp.exp(total - cum), e_la)
    dec_all = _dot_split_lhs(jnp.exp(total), e_la)
    xdec_b = (xdt * dec_end).astype(BF16)

    cm = c_ref[0]
    bm = b_ref[0]
    lane = lax.broadcasted_iota(jnp.int32, (CHUNK, LANES), 1)
    low = lane < SSD_HEAD_DIM
    heads_per_group = n_heads // SSD_GROUPS
    for g in range(SSD_GROUPS):
        cg = cm[:, g * SSD_STATE:(g + 1) * SSD_STATE]
        bg = bm[:, g * SSD_STATE:(g + 1) * SSD_STATE]
        scores = _dot_nt(cg, bg)
        cols = slice(g * gw, (g + 1) * gw)
        h_enter = state_ref[:, cols]
        y_inter = _dot(cg, h_enter.astype(BF16)) * dec_in[:, cols]
        state_ref[:, cols] = h_enter * dec_all[:, cols] + _dot_tn(bg, xdec_b[:, cols])
        for pair in range(heads_per_group // 2):
            acc = None
            base = g * gw + pair * LANES
            rhs = xdt_b[:, base:base + LANES]
            for k in range(2):
                h = g * heads_per_group + pair * 2 + k
                seg = cum[:, off_la + h:off_la + h + 1] - cum_t[off_la + h:off_la + h + 1, :]
                m = (scores * jnp.exp(jnp.where(before, seg, NEG_BIG))).astype(BF16)
                keep = low if k == 0 else jnp.logical_not(low)
                part = _dot(m, jnp.where(keep, rhs, jnp.zeros_like(rhs)))
                acc = part if acc is None else acc + part
            y = acc + y_inter[:, pair * LANES:(pair + 1) * LANES]
            if skip is not None:
                y = y + xs[:, base:base + LANES] * skip[:, base:base + LANES]
            y_ref[0, :, base:base + LANES] = y.astype(y_ref.dtype)


def _ssd_scan_kernel(xs_f, b_f, c_f, dtla_f, xs_b, b_b, c_b, dtla_b, skip_ref,
                     yf_ref, yb_ref, st_f, st_b):
    @pl.when(pl.program_id(1) == 0)
    def _():
        st_f[...] = jnp.zeros_like(st_f)
        st_b[...] = jnp.zeros_like(st_b)

    _ssd_chunk(xs_f, b_f, c_f, dtla_f, skip_ref[...], st_f, yf_ref, reverse=False)
    _ssd_chunk(xs_b, b_b, c_b, dtla_b, None, st_b, yb_ref, reverse=True)


def _ssd_scan(xbc, dtla, skip, hp, ctx_chunks):
    n_batch, s_len, _ = xbc.shape
    n_chunks = s_len // CHUNK
    bc_w = SSD_GROUPS * SSD_STATE
    b_blk = hp // bc_w
    fwd = lambda s: s
    bwd = lambda s: _bwd_chunk(s, ctx_chunks, n_chunks)

    def specs(order):
        return [
            pl.BlockSpec((1, CHUNK, hp), lambda b, s: (b, order(s), 0)),
            pl.BlockSpec((1, CHUNK, bc_w), lambda b, s: (b, order(s), b_blk)),
            pl.BlockSpec((1, CHUNK, bc_w), lambda b, s: (b, order(s), b_blk + 1)),
            pl.BlockSpec((1, CHUNK, LANES), lambda b, s: (b, order(s), 0)),
        ]

    return pl.pallas_call(
        _ssd_scan_kernel,
        out_shape=[jax.ShapeDtypeStruct((n_batch, s_len, hp), BF16)] * 2,
        grid=(n_batch, n_chunks),
        in_specs=specs(fwd) + specs(bwd) + [pl.BlockSpec((1, hp), lambda b, s: (0, 0))],
        out_specs=[pl.BlockSpec((1, CHUNK, hp), lambda b, s: (b, fwd(s), 0)),
                   pl.BlockSpec((1, CHUNK, hp), lambda b, s: (b, bwd(s), 0))],
        scratch_shapes=[pltpu.VMEM((SSD_STATE, hp), F32), pltpu.VMEM((SSD_STATE, hp), F32)],
        compiler_params=_cparams(("parallel", "arbitrary")),
    )(xbc, xbc, xbc, dtla, xbc, xbc, xbc, dtla, skip)


def _ret_chunk(q_ref, k_ref, v_ref, lg_ref, state_ref, y_ref, *, reverse):
    qk = q_ref.shape[2] // RET_HEADS
    vd = v_ref.shape[2] // RET_HEADS
    row = lax.broadcasted_iota(jnp.int32, (CHUNK, CHUNK), 0)
    col = lax.broadcasted_iota(jnp.int32, (CHUNK, CHUNK), 1)
    before = (col >= row) if reverse else (col <= row)
    dist = jnp.abs(row - col).astype(F32)
    n_in = ((CHUNK - row) if reverse else (row + 1)).astype(F32)
    n_end = ((row) if reverse else (CHUNK - 1 - row)).astype(F32)
    q = q_ref[0]
    k = k_ref[0]
    v = v_ref[0]
    d_sel = 1 if reverse else 0
    for h in range(RET_HEADS):
        lg = lg_ref[d_sel, h]
        qh = q[:, h * qk:(h + 1) * qk]
        kh = k[:, h * qk:(h + 1) * qk]
        vh = v[:, h * vd:(h + 1) * vd]
        scores = _dot_nt(qh, kh)
        m = (scores * jnp.exp(jnp.where(before, dist * lg, NEG_BIG))).astype(BF16)
        y_intra = _dot(m, vh)
        h_enter = state_ref[h]
        r_in = jnp.exp(n_in * lg)
        y_inter = _dot(qh, h_enter.astype(BF16))
        scale = jnp.concatenate([r_in] * (vd // CHUNK), axis=1)
        y_ref[0, :, h * vd:(h + 1) * vd] = (y_intra + y_inter * scale).astype(y_ref.dtype)
        r_end = jnp.exp(n_end * lg)
        k_scale = jnp.concatenate([r_end] * (qk // CHUNK), axis=1)
        kdec = (kh.astype(F32) * k_scale).astype(BF16)
        state_ref[h] = h_enter * jnp.exp(lg * CHUNK) + _dot_tn(kdec, vh)


def _ret_scan_kernel(lg_ref, q_f, k_f, v_f, q_b, k_b, v_b, yf_ref, yb_ref, st_f, st_b):
    @pl.when(pl.program_id(1) == 0)
    def _():
        st_f[...] = jnp.zeros_like(st_f)
        st_b[...] = jnp.zeros_like(st_b)

    _ret_chunk(q_f, k_f, v_f, lg_ref, st_f, yf_ref, reverse=False)
    _ret_chunk(q_b, k_b, v_b, lg_ref, st_b, yb_ref, reverse=True)


def _ret_scan(q, k, v, log_gamma, ctx_chunks):
    n_batch, s_len, qk_tot = q.shape
    v_tot = v.shape[2]
    n_chunks = s_len // CHUNK
    fwd = lambda s: s
    bwd = lambda s: _bwd_chunk(s, ctx_chunks, n_chunks)

    def specs(order):
        return [
            pl.BlockSpec((1, CHUNK, qk_tot), lambda b, s: (b, order(s), 0)),
            pl.BlockSpec((1, CHUNK, qk_tot), lambda b, s: (b, order(s), 0)),
            pl.BlockSpec((1, CHUNK, v_tot), lambda b, s: (b, order(s), 0)),
        ]

    return pl.pallas_call(
        _ret_scan_kernel,
        out_shape=[jax.ShapeDtypeStruct((n_batch, s_len, v_tot), BF16)] * 2,
        grid=(n_batch, n_chunks),
        in_specs=[pl.BlockSpec(memory_space=pltpu.SMEM)] + specs(fwd) + specs(bwd),
        out_specs=[pl.BlockSpec((1, CHUNK, v_tot), lambda b, s: (b, fwd(s), 0)),
                   pl.BlockSpec((1, CHUNK, v_tot), lambda b, s: (b, bwd(s), 0))],
        scratch_shapes=[pltpu.VMEM((RET_HEADS, qk_tot // RET_HEADS, v_tot // RET_HEADS), F32)] * 2,
        compiler_params=_cparams(("parallel", "arbitrary")),
    )(log_gamma, q, k, v, q, k, v)


def _readout_kernel(yf_ref, yb_ref, gate_ref, ng_ref, w_ref, x_ref, mod_ref, out_ref, *,
                    gate_first, group, gate_idx):
    y = yf_ref[0].astype(F32) + yb_ref[0].astype(F32)
    gate = _silu(gate_ref[0].astype(F32))
    if gate_first:
        y = y * gate
    pieces = []
    for j in range(y.shape[1] // group):
        yg = y[:, j * group:(j + 1) * group]
        ms = jnp.mean(yg * yg, axis=-1, keepdims=True)
        pieces.append(yg * lax.rsqrt(ms + NORM_EPS))
    y = jnp.concatenate(pieces, axis=1)
    y = y * ng_ref[...] if gate_first else y * gate
    o = _dot(y.astype(BF16), w_ref[...])
    out_ref[0] = x_ref[0] + mod_ref[0][gate_idx:gate_idx + 1] * o


def _readout(y_f, y_b, gate, norm_g, w_out, x, mod, ctx_tiles, gate_first, group):
    n_batch, s_len, inner = y_f.shape
    d = x.shape[2]
    n_tiles = s_len // ROW_TILE
    row = lambda b, t: (b, t, 0)
    return pl.pallas_call(
        functools.partial(_readout_kernel, gate_first=gate_first, group=group, gate_idx=2),
        out_shape=jax.ShapeDtypeStruct(x.shape, F32),
        grid=(n_batch, n_tiles),
        in_specs=[
            pl.BlockSpec((1, ROW_TILE, inner), row),
            pl.BlockSpec((1, ROW_TILE, inner), row),
            pl.BlockSpec((1, ROW_TILE, inner), row),
            pl.BlockSpec((1, inner), lambda b, t: (0, 0)),
            pl.BlockSpec((inner, d), lambda b, t: (0, 0)),
            pl.BlockSpec((1, ROW_TILE, d), row),
            pl.BlockSpec((1, 6, d), lambda b, t: (jnp.where(t < ctx_tiles, n_batch, b), 0, 0)),
        ],
        out_specs=pl.BlockSpec((1, ROW_TILE, d), row),
        compiler_params=_cparams(("parallel", "arbitrary")),
    )(y_f, y_b, gate, norm_g.reshape(1, inner), w_out, x, mod)


def _router_kernel(x_ref, g_ref, mod_ref, wr_ref, br_ref, h_ref, idx_ref, gate_ref, rank_ref,
                   count_ref, base_ref):
    @pl.when(jnp.logical_and(pl.program_id(0) == 0, pl.program_id(1) == 0))
    def _():
        base_ref[...] = jnp.zeros_like(base_ref)

    h = _norm_modulate(x_ref[0], g_ref[...], mod_ref[0], 3, 4)
    h_ref[0] = h
    h_hi, h_lo = _split_bf16(h)
    w_hi, w_lo = _split_bf16(wr_ref[...])
    logits = _dot(h_hi, w_hi) + _dot(h_lo, w_hi) + _dot(h_hi, w_lo) + br_ref[...]
    lane = lax.broadcasted_iota(jnp.int32, logits.shape, 1).astype(F32)
    work = logits
    idx_out = jnp.zeros(logits.shape, F32)
    val_out = jnp.full(logits.shape, NEG_BIG, F32)
    picks = []
    for k in range(TOP_K):
        m = jnp.max(work, axis=-1, keepdims=True)
        first = jnp.min(jnp.where(work == m, lane, float(LANES)), axis=-1, keepdims=True)
        pick = lane == first
        picks.append(pick)
        idx_out = jnp.where(lane == k, first, idx_out)
        val_out = jnp.where(lane == k, m, val_out)
        work = jnp.where(pick, -jnp.inf, work)
    top = jnp.max(val_out, axis=-1, keepdims=True)
    e = jnp.exp(val_out - top)
    idx_ref[0] = idx_out.astype(jnp.int32)
    gate_ref[0] = e / jnp.sum(e, axis=-1, keepdims=True)

    rows = logits.shape[0]
    onehot = jnp.zeros(logits.shape, F32)
    for pick in picks:
        onehot = onehot + jnp.where(pick, 1.0, 0.0)
    r = lax.broadcasted_iota(jnp.int32, (rows, rows), 0)
    c = lax.broadcasted_iota(jnp.int32, (rows, rows), 1)
    strict = jnp.where(c < r, 1.0, 0.0).astype(BF16)
    earlier = _dot(strict, onehot.astype(BF16)) + base_ref[...]
    rank_out = jnp.zeros(logits.shape, F32)
    for k, pick in enumerate(picks):
        rk = jnp.sum(jnp.where(pick, earlier, 0.0), axis=-1, keepdims=True)
        rank_out = jnp.where(lane == k, rk, rank_out)
    rank_ref[0] = rank_out.astype(jnp.int32)
    base_ref[...] = base_ref[...] + jnp.sum(onehot, axis=0, keepdims=True)
    count_ref[...] = base_ref[...].astype(jnp.int32)


def _router(x, g, mod, w_router, b_router, ctx_tiles):
    n_batch, s_len, d = x.shape
    n_exp = w_router.shape[1]
    wr = jnp.zeros((d, LANES), F32).at[:, :n_exp].set(w_router)
    br = jnp.full((1, LANES), NEG_BIG, F32).at[0, :n_exp].set(b_router)
    row = lambda b, t: (b, t, 0)
    return pl.pallas_call(
        _router_kernel,
        out_shape=[jax.ShapeDtypeStruct((n_batch, s_len, d), F32),
                   jax.ShapeDtypeStruct((n_batch, s_len, LANES), jnp.int32),
                   jax.ShapeDtypeStruct((n_batch, s_len, LANES), F32),
                   jax.ShapeDtypeStruct((n_batch, s_len, LANES), jnp.int32),
                   jax.ShapeDtypeStruct((1, LANES), jnp.int32)],
        grid=(n_batch, s_len // ROW_TILE),
        in_specs=[
            pl.BlockSpec((1, ROW_TILE, d), row),
            pl.BlockSpec((1, d), lambda b, t: (0, 0)),
            pl.BlockSpec((1, 6, d), lambda b, t: (jnp.where(t < ctx_tiles, n_batch, b), 0, 0)),
            pl.BlockSpec((d, LANES), lambda b, t: (0, 0)),
            pl.BlockSpec((1, LANES), lambda b, t: (0, 0)),
        ],
        out_specs=[pl.BlockSpec((1, ROW_TILE, d), row),
                   pl.BlockSpec((1, ROW_TILE, LANES), row),
                   pl.BlockSpec((1, ROW_TILE, LANES), row),
                   pl.BlockSpec((1, ROW_TILE, LANES), row),
                   pl.BlockSpec((1, LANES), lambda b, t: (0, 0))],
        scratch_shapes=[pltpu.VMEM((1, LANES), F32)],
        compiler_params=_cparams(("arbitrary", "arbitrary")),
    )(x, g.reshape(1, d), mod, wr, br)


def _on_slot(parity_of, fn):
    for s in range(2):
        @pl.when(parity_of % 2 == s)
        def _(s=s):
            fn(s)


def _indexed_row_pipeline(i, n, idx_hbm, idx_smem, isem, issue_rows):
    def idx_copy(tile, slot):
        return pltpu.make_async_copy(idx_hbm.at[tile], idx_smem[slot], isem.at[slot])

    @pl.when(i == 0)
    def _():
        idx_copy(0, 0).start()
        idx_copy(0, 0).wait()
        issue_rows(0, 0)

        @pl.when(n > 1)
        def _():
            idx_copy(1, 1).start()

    @pl.when(i + 1 < n)
    def _():
        def nxt(s):
            idx_copy(i + 1, s).wait()
            issue_rows(i + 1, s)
        _on_slot(i + 1, nxt)

    @pl.when(i + 2 < n)
    def _():
        _on_slot(i + 2, lambda s: idx_copy(i + 2, s).start())


def _dispatch_kernel(zblk_ref, nused_ref, pos_hbm, h_ref, xs_hbm, zero_buf, hbuf, idx_a, idx_b, zsem,
                     gsem, isem):
    i = pl.program_id(0)
    n = pl.num_programs(0)
    n_rows = TOP_K * ROW_TILE

    @pl.when(i == 0)
    def _():
        zero_buf[...] = jnp.zeros_like(zero_buf)

        def zero_block(blk):
            cp = pltpu.make_async_copy(
                zero_buf, xs_hbm.at[pl.ds(blk * EXPERT_ROWS, EXPERT_ROWS), :], zsem)
            cp.start()
            cp.wait()

        for e in range(zblk_ref.shape[0]):
            @pl.when(zblk_ref[e] >= 0)
            def _(e=e):
                zero_block(zblk_ref[e])

        def tail(blk, carry):
            zero_block(blk)
            return carry
        lax.fori_loop(nused_ref[0], xs_hbm.shape[0] // EXPERT_ROWS, tail, 0)

    def idx_copy(tile, slot):
        return pltpu.make_async_copy(pos_hbm.at[tile], (idx_a, idx_b)[slot], isem.at[slot])

    def wait_rows(slot):
        pltpu.make_async_copy(xs_hbm.at[pl.ds(0, n_rows), :], xs_hbm.at[pl.ds(0, n_rows), :],
                              gsem.at[slot]).wait()

    @pl.when(i == 0)
    def _():
        idx_copy(0, 0).start()

    def step(slot):
        idx = (idx_a, idx_b)[slot]
        idx_copy(i, slot).wait()

        @pl.when(i >= 2)
        def _():
            wait_rows(slot)

        hbuf[slot] = h_ref[...]

        def body(it, carry):
            for j in range(DMA_UNROLL):
                r = it * DMA_UNROLL + j
                pltpu.make_async_copy(hbuf.at[slot, pl.ds(r // TOP_K, 1), :],
                                      xs_hbm.at[pl.ds(idx[r], 1), :],
                                      gsem.at[slot]).start(priority=j % 2)
            return carry
        lax.fori_loop(0, n_rows // DMA_UNROLL, body, 0)
    _on_slot(i, step)

    @pl.when(i + 1 < n)
    def _():
        _on_slot(i + 1, lambda s: idx_copy(i + 1, s).start())

    @pl.when(i == n - 1)
    def _():
        @pl.when(n >= 2)
        def _():
            _on_slot(i + 1, wait_rows)
        _on_slot(i, wait_rows)


def _dispatch(pos_tiles, zero_blocks, n_used, h_flat, cap):
    n_tiles = pos_tiles.shape[0]
    d = h_flat.shape[1]
    grid_spec = pltpu.PrefetchScalarGridSpec(
        num_scalar_prefetch=2,
        grid=(n_tiles,),
        in_specs=[pl.BlockSpec(memory_space=pl.ANY),
                  pl.BlockSpec((ROW_TILE, d), lambda i, zb, nu: (i, 0))],
        out_specs=pl.BlockSpec(memory_space=pl.ANY),
        scratch_shapes=[
            pltpu.VMEM((EXPERT_ROWS, d), F32),
            pltpu.VMEM((2, ROW_TILE, d), F32),
            pltpu.SMEM((TOP_K * ROW_TILE,), jnp.int32),
            pltpu.SMEM((TOP_K * ROW_TILE,), jnp.int32),
            pltpu.SemaphoreType.DMA,
            pltpu.SemaphoreType.DMA((2,)),
            pltpu.SemaphoreType.DMA((2,)),
        ],
    )
    return pl.pallas_call(
        _dispatch_kernel,
        out_shape=jax.ShapeDtypeStruct((cap, d), F32),
        grid_spec=grid_spec,
        compiler_params=_cparams(("arbitrary",)),
    )(zero_blocks, n_used, pos_tiles, h_flat)


def _expert_kernel(bexp_ref, nused_ref, x_ref, w1_ref, b1_ref, w2_ref, b2_ref, out_ref, w1p, w2p):
    i = pl.program_id(0)
    prev = bexp_ref[jnp.maximum(i - 1, 0)]
    f2 = w1_ref.shape[2]

    @pl.when(jnp.logical_or(i == 0, bexp_ref[i] != prev))
    def _():
        src = lax.broadcasted_iota(jnp.int32, (MXU_DIM, MXU_DIM), 0)
        dst = lax.broadcasted_iota(jnp.int32, (MXU_DIM, MXU_DIM), 1)
        want = jnp.where(dst < LANES, 2 * dst, 2 * (dst - LANES) + 1)
        perm = jnp.where(src == want, 1.0, 0.0).astype(BF16)
        for c in range(f2 // MXU_DIM):
            blk = w1_ref[0, :, c * MXU_DIM:(c + 1) * MXU_DIM].astype(BF16)
            w1p[:, c * MXU_DIM:(c + 1) * MXU_DIM] = _dot(blk, perm).astype(BF16)
        w2p[...] = w2_ref[0].astype(BF16)

    @pl.when(i < nused_ref[0])
    def _():
        x = x_ref[...].astype(BF16)
        hdn = _dot(x, w1p[...]) + b1_ref[0]
        acts = []
        for c in range(f2 // MXU_DIM):
            glu = jnp.minimum(hdn[:, c * MXU_DIM:c * MXU_DIM + LANES], SWIGLU_LIMIT)
            lin = jnp.clip(hdn[:, c * MXU_DIM + LANES:(c + 1) * MXU_DIM], -SWIGLU_LIMIT, SWIGLU_LIMIT)
            acts.append(glu * _sigmoid(SWIGLU_ALPHA * glu) * (lin + 1.0))
        act = jnp.concatenate(acts, axis=1).astype(BF16)
        out_ref[...] = _dot(act, w2p[...]) + b2_ref[0]

    @pl.when(i >= nused_ref[0])
    def _():
        out_ref[...] = jnp.zeros_like(out_ref)


def _experts(x_sorted, block_expert, n_used, w1, b1p, w2, b2):
    n_blocks = block_expert.shape[0]
    d = x_sorted.shape[1]
    n_exp, _, f2 = w1.shape
    f = w2.shape[1]
    wspec = lambda shape: pl.BlockSpec(shape, lambda i, be, nu: (be[i], 0, 0))
    grid_spec = pltpu.PrefetchScalarGridSpec(
        num_scalar_prefetch=2,
        grid=(n_blocks,),
        in_specs=[
            pl.BlockSpec((EXPERT_ROWS, d), lambda i, be, nu: (jnp.minimum(i, nu[0] - 1), 0)),
            wspec((1, d, f2)), wspec((1, 1, f2)), wspec((1, f, d)), wspec((1, 1, d)),
        ],
        out_specs=pl.BlockSpec((EXPERT_ROWS, d), lambda i, be, nu: (i, 0)),
        scratch_shapes=[pltpu.VMEM((d, f2), BF16), pltpu.VMEM((f, d), BF16)],
    )
    return pl.pallas_call(
        _expert_kernel,
        out_shape=jax.ShapeDtypeStruct((n_blocks * EXPERT_ROWS, d), F32),
        grid_spec=grid_spec,
        compiler_params=_cparams(("arbitrary",)),
    )(block_expert, n_used, x_sorted, w1, b1p, w2, b2.reshape(n_exp, 1, d))


def _combine_kernel(pos_hbm, y_hbm, gate_ref, x_ref, mod_ref, out_ref, ybuf, idx_a, idx_b, gsem, isem):
    i = pl.program_id(0)
    n = pl.num_programs(0)
    n_rows = TOP_K * ROW_TILE

    def issue_rows(tile, slot):
        idx = (idx_a, idx_b)[slot]

        def body(it, carry):
            for j in range(DMA_UNROLL):
                r = it * DMA_UNROLL + j
                pltpu.make_async_copy(y_hbm.at[pl.ds(idx[r], 1), :], ybuf.at[slot, pl.ds(r, 1), :],
                                      gsem.at[slot]).start(priority=j % 2)
            return carry
        lax.fori_loop(0, n_rows // DMA_UNROLL, body, 0)

    def wait_rows(slot):
        pltpu.make_async_copy(y_hbm.at[pl.ds(0, n_rows), :], ybuf.at[slot], gsem.at[slot]).wait()

    _indexed_row_pipeline(i, n, pos_hbm, (idx_a, idx_b), isem, issue_rows)
    _on_slot(i, wait_rows)

    def compute(s):
        gate = gate_ref[...]
        acc = None
        for k in range(TOP_K):
            part = ybuf[s, k * ROW_TILE:(k + 1) * ROW_TILE, :] * gate[:, k:k + 1]
            acc = part if acc is None else acc + part
        out_ref[...] = x_ref[...] + mod_ref[0][5:6] * acc
    _on_slot(i, compute)


def _combine(pos_tiles, y_rows, gate_flat, x_flat, mod, n_batch, tiles_per_batch, ctx_tiles):
    n_tiles = pos_tiles.shape[0]
    d = x_flat.shape[1]

    def mod_map(i):
        b = i // tiles_per_batch
        return (jnp.where(i % tiles_per_batch < ctx_tiles, n_batch, b), 0, 0)

    return pl.pallas_call(
        _combine_kernel,
        out_shape=jax.ShapeDtypeStruct(x_flat.shape, F32),
        grid=(n_tiles,),
        in_specs=[
            pl.BlockSpec(memory_space=pl.ANY),
            pl.BlockSpec(memory_space=pl.ANY),
            pl.BlockSpec((ROW_TILE, LANES), lambda i: (i, 0)),
            pl.BlockSpec((ROW_TILE, d), lambda i: (i, 0)),
            pl.BlockSpec((1, 6, d), mod_map),
        ],
        out_specs=pl.BlockSpec((ROW_TILE, d), lambda i: (i, 0)),
        scratch_shapes=[
            pltpu.VMEM((2, TOP_K * ROW_TILE, d), F32),
            pltpu.SMEM((TOP_K * ROW_TILE,), jnp.int32),
            pltpu.SMEM((TOP_K * ROW_TILE,), jnp.int32),
            pltpu.SemaphoreType.DMA((2,)),
            pltpu.SemaphoreType.DMA((2,)),
        ],
        compiler_params=_cparams(("arbitrary",)),
    )(pos_tiles, y_rows, gate_flat, x_flat, mod)


def _moe(x, g, mod, w_router, b_router, w1, b1p, w2, b2, ctx_tiles):
    n_batch, s_len, d = x.shape
    n_tok = n_batch * s_len
    n_exp = w1.shape[0]
    n_tiles = n_tok // ROW_TILE
    h, idx, gate, rank, count = _router(x, g, mod, w_router, b_router, ctx_tiles)

    counts = count[0, :n_exp]
    padded = (counts + EXPERT_ROWS - 1) // EXPERT_ROWS * EXPERT_ROWS
    pend = jnp.cumsum(padded)
    pstart = pend - padded
    n_blocks = -(-(n_tok * TOP_K) // EXPERT_ROWS) + n_exp
    block_start = jnp.arange(n_blocks, dtype=jnp.int32) * EXPERT_ROWS
    block_expert = jnp.minimum(jnp.sum(block_start[:, None] >= pend[None, :], axis=1),
                               n_exp - 1).astype(jnp.int32)
    n_used = (pend[-1] // EXPERT_ROWS).astype(jnp.int32).reshape(1)
    zero_blocks = jnp.where(padded > counts, pend // EXPERT_ROWS - 1, -1).astype(jnp.int32)

    top_i = idx[..., :TOP_K].reshape(n_tok, TOP_K)
    pos = (pstart[top_i] + rank[..., :TOP_K].reshape(n_tok, TOP_K)).astype(jnp.int32)
    pos_token_major = pos.reshape(n_tiles, ROW_TILE * TOP_K)
    pos_slot_major = pos.reshape(n_tiles, ROW_TILE, TOP_K).transpose(0, 2, 1).reshape(n_tiles, -1)

    x_sorted = _dispatch(pos_token_major, zero_blocks, n_used, h.reshape(n_tok, d),
                         n_blocks * EXPERT_ROWS)
    y_rows = _experts(x_sorted, block_expert, n_used, w1, b1p, w2, b2)
    out = _combine(pos_slot_major, y_rows, gate.reshape(n_tok, LANES), x.reshape(n_tok, d), mod,
                   n_batch, s_len // ROW_TILE, ctx_tiles)
    return out.reshape(n_batch, s_len, d)


def _final_norm_kernel(x_ref, g_ref, out_ref):
    x = x_ref[0]
    ms = jnp.mean(x * x, axis=-1, keepdims=True)
    out_ref[0] = x * lax.rsqrt(ms + NORM_EPS) * g_ref[...]


def _final_norm(x, g, ctx_tiles, seq_len):
    n_batch, _, d = x.shape
    return pl.pallas_call(
        _final_norm_kernel,
        out_shape=jax.ShapeDtypeStruct((n_batch, seq_len, d), F32),
        grid=(n_batch, seq_len // ROW_TILE),
        in_specs=[pl.BlockSpec((1, ROW_TILE, d), lambda b, t: (b, t + ctx_tiles, 0)),
                  pl.BlockSpec((1, d), lambda b, t: (0, 0))],
        out_specs=pl.BlockSpec((1, ROW_TILE, d), lambda b, t: (b, t, 0)),
        compiler_params=_cparams(("parallel", "arbitrary")),
    )(x, g.reshape(1, d))


def _rope_tables(seq_len, ctx_len, half_dim):
    rows = seq_len // GRID_W
    t = jnp.arange(seq_len)
    row_pos = (t // GRID_W).astype(F32) - (rows - 1) / 2.0
    col_pos = (t % GRID_W).astype(F32) - (GRID_W - 1) / 2.0
    inv_freq = ROPE_BASE ** (-jnp.arange(0, half_dim, 2, dtype=F32) / half_dim)

    def tables(pos):
        ang = pos[:, None] * inv_freq[None, :]
        c, s = jnp.cos(ang), jnp.sin(ang)
        return jnp.concatenate([c, c], axis=1), jnp.concatenate([-s, s], axis=1)

    cr, sr = tables(row_pos)
    cc, sc = tables(col_pos)
    cos = jnp.concatenate([cr, cc], axis=1)
    sin = jnp.concatenate([sr, sc], axis=1)
    cos = jnp.concatenate([jnp.ones((ctx_len, cos.shape[1]), F32), cos], axis=0)
    sin = jnp.concatenate([jnp.zeros((ctx_len, sin.shape[1]), F32), sin], axis=0)
    return cos, sin


def kernel(x, c, ctx, c_ctx, w_mod, b_mod, norm1_g, norm2_g, ssd_w_in, ssd_conv_w, ssd_conv_b,
           ssd_dt_bias_f, ssd_dt_bias_b, ssd_a_log_f, ssd_a_log_b, ssd_d, ssd_norm_g, ssd_w_out,
           ret_w_in, ret_scale_f, ret_scale_b, ret_w_out, router_w, router_b,
           moe_w1, moe_b1, moe_w2, moe_b2, final_g):
    n_batch, seq_len, d = x.shape
    ctx_len = ctx.shape[1]
    depth = w_mod.shape[0]
    assert ctx_len % ROW_TILE == 0 and seq_len % ROW_TILE == 0
    ctx_tiles = ctx_len // ROW_TILE
    ctx_chunks = ctx_len // CHUNK

    n_heads = ssd_d.shape[1]
    ssd_inner = n_heads * SSD_HEAD_DIM
    bc_w = SSD_GROUPS * SSD_STATE
    conv_dim = ssd_inner + 2 * bc_w
    assert n_heads == LANES // 4 and ssd_inner % bc_w == 0

    ret_qk = (ret_w_in.shape[2] - 2 * ret_w_out.shape[1]) // 2
    ret_v = ret_w_out.shape[1]
    qk_dim = ret_qk // RET_HEADS
    assert qk_dim == 2 * LANES

    n_exp, _, f2 = moe_w1.shape[1:]
    assert f2 % MXU_DIM == 0

    cond = jnp.concatenate([c, c_ctx[None, :], jnp.zeros((16 - n_batch - 1, d), F32)], axis=0)
    mod_all = _modulation(cond, w_mod, b_mod).reshape(depth, 16, 6, d)

    xs = jnp.concatenate([ctx, x], axis=1)
    cos, sin = _rope_tables(seq_len, ctx_len, qk_dim // 2)

    for i in range(depth):
        j = i // 2
        mod = mod_all[i]
        if i % 2 == 0:
            w_in = ssd_w_in[j]
            w_main = w_in[:, :ssd_inner + conv_dim].astype(BF16)
            w_dt = w_in[:, ssd_inner + conv_dim:].astype(BF16)
            w_cat = jnp.concatenate([w_main, w_dt, w_dt], axis=1)
            splits = [(0, ssd_inner), (ssd_inner, conv_dim), (ssd_inner + conv_dim, LANES)]
            z, xbc, dt_raw = _in_proj(xs, norm1_g[i], mod, w_cat, splits, [BF16, BF16, F32], ctx_tiles)
            dt_bias2 = jnp.concatenate([ssd_dt_bias_f[j], ssd_dt_bias_b[j]])
            dt_bias4 = jnp.concatenate([dt_bias2, dt_bias2]).reshape(1, LANES)
            a_log4 = jnp.concatenate([jnp.zeros((2 * n_heads,), F32), ssd_a_log_f[j],
                                      ssd_a_log_b[j]]).reshape(1, LANES)
            xbc_c, dtla = _ssd_conv(xbc, dt_raw, ssd_conv_w[j], ssd_conv_b[j], dt_bias4, a_log4,
                                    ctx_chunks)
            skip = jnp.repeat(ssd_d[j], SSD_HEAD_DIM).reshape(1, ssd_inner)
            y_f, y_b = _ssd_scan(xbc_c, dtla, skip, ssd_inner, ctx_chunks)
            xs = _readout(y_f, y_b, z, ssd_norm_g[j], ssd_w_out[j].astype(BF16), xs, mod,
                          ctx_tiles, True, ssd_inner // SSD_GROUPS)
        else:
            w_cat = ret_w_in[j].astype(BF16)
            splits = [(0, ret_qk), (ret_qk, ret_qk), (2 * ret_qk, ret_v), (2 * ret_qk + ret_v, ret_v)]
            q, k, v, gt = _in_proj(xs, norm1_g[i], mod, w_cat, splits, [BF16] * 4, ctx_tiles,
                                   rope=(cos, sin, (0, 1)), k_scale=qk_dim ** -0.5)
            log_gamma = jnp.stack([jnp.log1p(-jnp.exp2(-ret_scale_f[j].astype(F32))),
                                   jnp.log1p(-jnp.exp2(-ret_scale_b[j].astype(F32)))])
            y_f, y_b = _ret_scan(q, k, v, log_gamma, ctx_chunks)
            xs = _readout(y_f, y_b, gt, jnp.ones((ret_v,), F32), ret_w_out[j].astype(BF16), xs, mod,
                          ctx_tiles, False, ret_v // RET_HEADS)
        b1p = moe_b1[i].reshape(n_exp, f2 // MXU_DIM, LANES, 2).transpose(0, 1, 3, 2).reshape(n_exp, 1, f2)
        xs = _moe(xs, norm2_g[i], mod, router_w[i], router_b[i], moe_w1[i], b1p, moe_w2[i], moe_b2[i],
                  ctx_tiles)
    return _final_norm(xs, final_g, ctx_tiles, seq_len)
```

```python
import functools

import jax
import jax.numpy as jnp
from jax import lax
from jax.experimental import pallas as pl
from jax.experimental.pallas import tpu as pltpu

F32 = jnp.float32
BF16 = jnp.bfloat16

NORM_EPS = 1e-6
CHUNK = 128
GRID_W = 64
ROPE_BASE = 10000.0

SSD_HEAD_DIM = 64
SSD_GROUPS = 4
SSD_STATE = 128
SSD_CONV = 5

RET_HEADS = 4

TOP_K = 4
SWIGLU_ALPHA = 1.702
SWIGLU_LIMIT = 7.0

LANES = 128
MXU_DIM = 256
ROW_TILE = 256
EXPERT_ROWS = 256
HALO = 16
TOKEN_SUBLANES = 8
DMA_UNROLL = 8
VMEM_LIMIT = 56 * 1024 * 1024
NEG_BIG = -1e30


def _cparams(sem):
    return pltpu.CompilerParams(dimension_semantics=sem, vmem_limit_bytes=VMEM_LIMIT)


def _split_bf16(a):
    hi = a.astype(BF16)
    lo = (a - hi.astype(F32)).astype(BF16)
    return hi, lo


def _dot(a, b):
    return jnp.dot(a, b, preferred_element_type=F32)


def _dot_tn(a, b):
    return lax.dot_general(a, b, (((0,), (0,)), ((), ())), preferred_element_type=F32)


def _dot_nt(a, b):
    return lax.dot_general(a, b, (((1,), (1,)), ((), ())), preferred_element_type=F32)


def _dot_split_lhs(a, b_bf16):
    hi, lo = _split_bf16(a)
    return _dot(hi, b_bf16) + _dot(lo, b_bf16)


def _store_token_tiles(ref, lead, value):
    rows = value.shape[0]
    for s in range(TOKEN_SUBLANES):
        ref[lead + (pl.ds(s, rows, stride=TOKEN_SUBLANES), slice(None))] = value[:, s * LANES:(s + 1) * LANES]


def _load_token_tiles(ref, lead, first_row, rows):
    return [ref[lead + (pl.ds(first_row + s, rows, stride=TOKEN_SUBLANES), slice(None))]
            for s in range(TOKEN_SUBLANES)]


def _sigmoid(x):
    return 1.0 / (1.0 + jnp.exp(-x))


def _silu(x):
    return x * _sigmoid(x)


def _mod_kernel(cond_ref, w_ref, b_ref, out_ref):
    s = _silu(cond_ref[...])
    s_hi, s_lo = _split_bf16(s)
    w_hi, w_lo = _split_bf16(w_ref[0])
    acc = _dot(s_hi, w_hi) + _dot(s_lo, w_hi) + _dot(s_hi, w_lo)
    out_ref[0] = acc + b_ref[0]


def _modulation(cond, w_mod, b_mod):
    depth, d, n = w_mod.shape
    rows = cond.shape[0]
    tn = 1024
    return pl.pallas_call(
        _mod_kernel,
        out_shape=jax.ShapeDtypeStruct((depth, rows, n), F32),
        grid=(depth, n // tn),
        in_specs=[
            pl.BlockSpec((rows, d), lambda l, j: (0, 0)),
            pl.BlockSpec((1, d, tn), lambda l, j: (l, 0, j)),
            pl.BlockSpec((1, 1, tn), lambda l, j: (l, 0, j)),
        ],
        out_specs=pl.BlockSpec((1, rows, tn), lambda l, j: (l, 0, j)),
        compiler_params=_cparams(("arbitrary", "arbitrary")),
    )(cond, w_mod, b_mod.reshape(depth, 1, n))


def _norm_modulate(x, g, mod, shift_idx, scale_idx):
    ms = jnp.mean(x * x, axis=-1, keepdims=True)
    h = x * lax.rsqrt(ms + NORM_EPS) * g
    return h * (1.0 + mod[scale_idx:scale_idx + 1]) + mod[shift_idx:shift_idx + 1]


def _in_proj_kernel(*refs, n_out, rope_outs, k_scale):
    x_ref, g_ref, mod_ref = refs[:3]
    w_refs = refs[3:3 + n_out]
    if rope_outs:
        cos_ref, sin_ref = refs[3 + n_out:5 + n_out]
        out_refs = refs[5 + n_out:]
    else:
        out_refs = refs[3 + n_out:]
    h = _norm_modulate(x_ref[0], g_ref[...], mod_ref[0], 0, 1).astype(BF16)
    for idx, w_ref in enumerate(w_refs):
        o = _dot(h, w_ref[...])
        if idx in rope_outs:
            cos = cos_ref[...]
            sin = sin_ref[...]
            pieces = []
            for j in range(o.shape[1] // LANES):
                u = o[:, j * LANES:(j + 1) * LANES]
                half = (j % 2) * LANES
                r = (u * cos[:, half:half + LANES]
                     + pltpu.roll(u, LANES // 2, axis=1) * sin[:, half:half + LANES])
                pieces.append(r)
            o = jnp.concatenate(pieces, axis=1)
            if idx == rope_outs[-1]:
                o = o * k_scale
        out_refs[idx][0] = o.astype(out_refs[idx].dtype)


def _in_proj(x, g, mod, weights, out_dtypes, ctx_tiles, rope=None, k_scale=1.0):
    n_batch, s_len, d = x.shape
    n_tiles = s_len // ROW_TILE
    in_specs = [
        pl.BlockSpec((1, ROW_TILE, d), lambda b, t: (b, t, 0)),
        pl.BlockSpec((1, d), lambda b, t: (0, 0)),
        pl.BlockSpec((1, 6, d), lambda b, t: (jnp.where(t < ctx_tiles, n_batch, b), 0, 0)),
    ] + [pl.BlockSpec(w.shape, lambda b, t: (0, 0)) for w in weights]
    args = [x, g.reshape(1, d), mod] + list(weights)
    splits = [(None, w.shape[1]) for w in weights]
    rope_outs = ()
    if rope is not None:
        cos, sin, rope_outs = rope
        in_specs += [pl.BlockSpec((ROW_TILE, cos.shape[1]), lambda b, t: (t, 0)),
                     pl.BlockSpec((ROW_TILE, sin.shape[1]), lambda b, t: (t, 0))]
        args += [cos, sin]
    out_shape = [jax.ShapeDtypeStruct((n_batch, s_len, wd), dt)
                 for (_, wd), dt in zip(splits, out_dtypes)]
    out_specs = [pl.BlockSpec((1, ROW_TILE, wd), lambda b, t: (b, t, 0)) for _, wd in splits]
    return pl.pallas_call(
        functools.partial(_in_proj_kernel, n_out=len(weights), rope_outs=tuple(rope_outs),
                          k_scale=k_scale),
        out_shape=out_shape,
        grid=(n_batch, n_tiles),
        in_specs=in_specs,
        out_specs=out_specs,
        compiler_params=_cparams(("parallel", "arbitrary")),
    )(*args)


def _ssd_conv_kernel(cur_ref, prev_ref, next_ref, dt_ref, cw_ref, cb_ref, dtb_ref, alog_ref,
                     xbc_ref, dtla_ref, ext_ref, *, ctx_chunks, n_chunks):
    c = pl.program_id(1)
    has_prev = jnp.logical_and(c != 0, c != ctx_chunks)
    has_next = jnp.logical_and(c != ctx_chunks - 1, c != n_chunks - 1)
    pad = SSD_CONV // 2
    ext_ref[0:8, :] = prev_ref[0].astype(F32)[HALO - 8:HALO, :] * jnp.where(has_prev, 1.0, 0.0)
    ext_ref[8:8 + CHUNK, :] = cur_ref[0].astype(F32)
    ext_ref[8 + CHUNK:16 + CHUNK, :] = next_ref[0].astype(F32)[0:8, :] * jnp.where(has_next, 1.0, 0.0)
    cw = cw_ref[...]
    acc = jnp.zeros((CHUNK, cur_ref.shape[2]), F32) + cb_ref[...]
    for k in range(SSD_CONV):
        acc = acc + ext_ref[8 - pad + k:8 - pad + k + CHUNK, :] * cw[k:k + 1, :]
    xbc_ref[0] = _silu(acc).astype(xbc_ref.dtype)
    raw = dt_ref[0] + dtb_ref[...]
    dt = jnp.maximum(raw, 0.0) + jnp.log(1.0 + jnp.exp(-jnp.abs(raw)))
    lane = lax.broadcasted_iota(jnp.int32, dt.shape, 1)
    a = -jnp.exp(alog_ref[...])
    dtla_ref[0] = jnp.where(lane < LANES // 2, dt, dt * a)


def _ssd_conv(xbc, dt_raw, conv_w, conv_b, dt_bias4, a_log4, ctx_chunks):
    n_batch, s_len, cdim = xbc.shape
    n_chunks = s_len // CHUNK
    per = CHUNK // HALO
    last_halo = s_len // HALO - 1
    return pl.pallas_call(
        functools.partial(_ssd_conv_kernel, ctx_chunks=ctx_chunks, n_chunks=n_chunks),
        out_shape=[jax.ShapeDtypeStruct((n_batch, s_len, cdim), BF16),
                   jax.ShapeDtypeStruct((n_batch, s_len, LANES), F32)],
        grid=(n_batch, n_chunks),
        in_specs=[
            pl.BlockSpec((1, CHUNK, cdim), lambda b, c: (b, c, 0)),
            pl.BlockSpec((1, HALO, cdim), lambda b, c: (b, jnp.maximum(c * per - 1, 0), 0)),
            pl.BlockSpec((1, HALO, cdim), lambda b, c: (b, jnp.minimum(c * per + per, last_halo), 0)),
            pl.BlockSpec((1, CHUNK, LANES), lambda b, c: (b, c, 0)),
            pl.BlockSpec((SSD_CONV, cdim), lambda b, c: (0, 0)),
            pl.BlockSpec((1, cdim), lambda b, c: (0, 0)),
            pl.BlockSpec((1, LANES), lambda b, c: (0, 0)),
            pl.BlockSpec((1, LANES), lambda b, c: (0, 0)),
        ],
        out_specs=[pl.BlockSpec((1, CHUNK, cdim), lambda b, c: (b, c, 0)),
                   pl.BlockSpec((1, CHUNK, LANES), lambda b, c: (b, c, 0))],
        scratch_shapes=[pltpu.VMEM((CHUNK + 16, cdim), F32)],
        compiler_params=_cparams(("parallel", "arbitrary")),
    )(xbc, xbc, xbc, dt_raw, conv_w, conv_b.reshape(1, cdim), dt_bias4, a_log4)


def _bwd_chunk(s, ctx_chunks, n_chunks):
    return jnp.where(s < ctx_chunks, ctx_chunks - 1 - s, n_chunks - 1 - (s - ctx_chunks))


def _ssd_chunk(xs_ref, b_ref, c_ref, dtla_ref, skip, state_ref, y_ref, *, reverse):
    n_heads = xs_ref.shape[2] // SSD_HEAD_DIM
    hp = xs_ref.shape[2]
    gw = hp // SSD_GROUPS
    off_dt = LANES // 4 if reverse else 0
    off_la = off_dt + LANES // 2

    row = lax.broadcasted_iota(jnp.int32, (CHUNK, CHUNK), 0)
    col = lax.broadcasted_iota(jnp.int32, (CHUNK, CHUNK), 1)
    before = (col >= row) if reverse else (col <= row)
    tri = jnp.where(before, 1.0, 0.0).astype(BF16)
    tri_t = jnp.where((row >= col) if reverse else (row <= col), 1.0, 0.0).astype(BF16)

    dtla = dtla_ref[0]
    hi, lo = _split_bf16(dtla)
    cum = _dot(tri, hi) + _dot(tri, lo)
    cum_t = _dot_tn(hi, tri_t) + _dot_tn(lo, tri_t)
    cum = jnp.where(lax.broadcasted_iota(jnp.int32, cum.shape, 1) >= LANES // 2, cum, 0.0)
    last = 0 if reverse else CHUNK - 1
    total = cum[last:last + 1, :]

    sec = lax.broadcasted_iota(jnp.int32, (LANES, hp), 0)
    head = lax.broadcasted_iota(jnp.int32, (LANES, hp), 1) // SSD_HEAD_DIM
    e_dt = jnp.where(sec == head + off_dt, 1.0, 0.0).astype(BF16)
    e_la = jnp.where(sec == head + off_la, 1.0, 0.0).astype(BF16)

    xs = xs_ref[0].astype(F32)
    xdt = xs * _dot_split_lhs(dtla, e_dt)
    xdt_b = xdt.astype(BF16)
    dec_in = _dot_split_lhs(jnp.exp(cum), e_la)
    dec_end = _dot_split_lhs(jnp.exp(total - cum), e_la)
    dec_all = _dot_split_lhs(jnp.exp(total), e_la)
    xdec_b = (xdt * dec_end).astype(BF16)

    cm = c_ref[0]
    bm = b_ref[0]
    lane = lax.broadcasted_iota(jnp.int32, (CHUNK, LANES), 1)
    low = lane < SSD_HEAD_DIM
    heads_per_group = n_heads // SSD_GROUPS
    for g in range(SSD_GROUPS):
        cg = cm[:, g * SSD_STATE:(g + 1) * SSD_STATE]
        bg = bm[:, g * SSD_STATE:(g + 1) * SSD_STATE]
        scores = _dot_nt(cg, bg)
        cols = slice(g * gw, (g + 1) * gw)
        h_enter = state_ref[:, cols]
        y_inter = _dot(cg, h_enter.astype(BF16)) * dec_in[:, cols]
        state_ref[:, cols] = h_enter * dec_all[:, cols] + _dot_tn(bg, xdec_b[:, cols])
        for pair in range(heads_per_group // 2):
            acc = None
            base = g * gw + pair * LANES
            rhs = xdt_b[:, base:base + LANES]
            for k in range(2):
                h = g * heads_per_group + pair * 2 + k
                seg = cum[:, off_la + h:off_la + h + 1] - cum_t[off_la + h:off_la + h + 1, :]
                m = (scores * jnp.exp(jnp.where(before, seg, NEG_BIG))).astype(BF16)
                keep = low if k == 0 else jnp.logical_not(low)
                part = _dot(m, jnp.where(keep, rhs, jnp.zeros_like(rhs)))
                acc = part if acc is None else acc + part
            y = acc + y_inter[:, pair * LANES:(pair + 1) * LANES]
            if skip is not None:
                y = y + xs[:, base:base + LANES] * skip[:, base:base + LANES]
            y_ref[0, :, base:base + LANES] = y.astype(y_ref.dtype)


def _ssd_scan_kernel(xs_f, b_f, c_f, dtla_f, xs_b, b_b, c_b, dtla_b, skip_ref,
                     yf_ref, yb_ref, st_f, st_b):
    @pl.when(pl.program_id(1) == 0)
    def _():
        st_f[...] = jnp.zeros_like(st_f)
        st_b[...] = jnp.zeros_like(st_b)

    _ssd_chunk(xs_f, b_f, c_f, dtla_f, skip_ref[...], st_f, yf_ref, reverse=False)
    _ssd_chunk(xs_b, b_b, c_b, dtla_b, None, st_b, yb_ref, reverse=True)


def _ssd_scan(xbc, dtla, skip, hp, ctx_chunks):
    n_batch, s_len, _ = xbc.shape
    n_chunks = s_len // CHUNK
    bc_w = SSD_GROUPS * SSD_STATE
    b_blk = hp // bc_w
    fwd = lambda s: s
    bwd = lambda s: _bwd_chunk(s, ctx_chunks, n_chunks)

    def specs(order):
        return [
            pl.BlockSpec((1, CHUNK, hp), lambda b, s: (b, order(s), 0)),
            pl.BlockSpec((1, CHUNK, bc_w), lambda b, s: (b, order(s), b_blk)),
            pl.BlockSpec((1, CHUNK, bc_w), lambda b, s: (b, order(s), b_blk + 1)),
            pl.BlockSpec((1, CHUNK, LANES), lambda b, s: (b, order(s), 0)),
        ]

    return pl.pallas_call(
        _ssd_scan_kernel,
        out_shape=[jax.ShapeDtypeStruct((n_batch, s_len, hp), BF16)] * 2,
        grid=(n_batch, n_chunks),
        in_specs=specs(fwd) + specs(bwd) + [pl.BlockSpec((1, hp), lambda b, s: (0, 0))],
        out_specs=[pl.BlockSpec((1, CHUNK, hp), lambda b, s: (b, fwd(s), 0)),
                   pl.BlockSpec((1, CHUNK, hp), lambda b, s: (b, bwd(s), 0))],
        scratch_shapes=[pltpu.VMEM((SSD_STATE, hp), F32), pltpu.VMEM((SSD_STATE, hp), F32)],
        compiler_params=_cparams(("parallel", "arbitrary")),
    )(xbc, xbc, xbc, dtla, xbc, xbc, xbc, dtla, skip)


def _ret_chunk(q_ref, k_ref, v_ref, lg_ref, state_ref, y_ref, *, reverse):
    qk = q_ref.shape[2] // RET_HEADS
    vd = v_ref.shape[2] // RET_HEADS
    row = lax.broadcasted_iota(jnp.int32, (CHUNK, CHUNK), 0)
    col = lax.broadcasted_iota(jnp.int32, (CHUNK, CHUNK), 1)
    before = (col >= row) if reverse else (col <= row)
    dist = jnp.abs(row - col).astype(F32)
    n_in = ((CHUNK - row) if reverse else (row + 1)).astype(F32)
    n_end = ((row) if reverse else (CHUNK - 1 - row)).astype(F32)
    q = q_ref[0]
    k = k_ref[0]
    v = v_ref[0]
    d_sel = 1 if reverse else 0
    for h in range(RET_HEADS):
        lg = lg_ref[d_sel, h]
        qh = q[:, h * qk:(h + 1) * qk]
        kh = k[:, h * qk:(h + 1) * qk]
        vh = v[:, h * vd:(h + 1) * vd]
        scores = _dot_nt(qh, kh)
        m = (scores * jnp.exp(jnp.where(before, dist * lg, NEG_BIG))).astype(BF16)
        y_intra = _dot(m, vh)
        h_enter = state_ref[h]
        r_in = jnp.exp(n_in * lg)
        y_inter = _dot(qh, h_enter.astype(BF16))
        scale = jnp.concatenate([r_in] * (vd // CHUNK), axis=1)
        y_ref[0, :, h * vd:(h + 1) * vd] = (y_intra + y_inter * scale).astype(y_ref.dtype)
        r_end = jnp.exp(n_end * lg)
        k_scale = jnp.concatenate([r_end] * (qk // CHUNK), axis=1)
        kdec = (kh.astype(F32) * k_scale).astype(BF16)
        state_ref[h] = h_enter * jnp.exp(lg * CHUNK) + _dot_tn(kdec, vh)


def _ret_scan_kernel(lg_ref, q_f, k_f, v_f, q_b, k_b, v_b, yf_ref, yb_ref, st_f, st_b):
    @pl.when(pl.program_id(1) == 0)
    def _():
        st_f[...] = jnp.zeros_like(st_f)
        st_b[...] = jnp.zeros_like(st_b)

    _ret_chunk(q_f, k_f, v_f, lg_ref, st_f, yf_ref, reverse=False)
    _ret_chunk(q_b, k_b, v_b, lg_ref, st_b, yb_ref, reverse=True)


def _ret_scan(q, k, v, log_gamma, ctx_chunks):
    n_batch, s_len, qk_tot = q.shape
    v_tot = v.shape[2]
    n_chunks = s_len // CHUNK
    fwd = lambda s: s
    bwd = lambda s: _bwd_chunk(s, ctx_chunks, n_chunks)

    def specs(order):
        return [
            pl.BlockSpec((1, CHUNK, qk_tot), lambda b, s: (b, order(s), 0)),
            pl.BlockSpec((1, CHUNK, qk_tot), lambda b, s: (b, order(s), 0)),
            pl.BlockSpec((1, CHUNK, v_tot), lambda b, s: (b, order(s), 0)),
        ]

    return pl.pallas_call(
        _ret_scan_kernel,
        out_shape=[jax.ShapeDtypeStruct((n_batch, s_len, v_tot), BF16)] * 2,
        grid=(n_batch, n_chunks),
        in_specs=[pl.BlockSpec(memory_space=pltpu.SMEM)] + specs(fwd) + specs(bwd),
        out_specs=[pl.BlockSpec((1, CHUNK, v_tot), lambda b, s: (b, fwd(s), 0)),
                   pl.BlockSpec((1, CHUNK, v_tot), lambda b, s: (b, bwd(s), 0))],
        scratch_shapes=[pltpu.VMEM((RET_HEADS, qk_tot // RET_HEADS, v_tot // RET_HEADS), F32)] * 2,
        compiler_params=_cparams(("parallel", "arbitrary")),
    )(log_gamma, q, k, v, q, k, v)


def _readout_kernel(yf_ref, yb_ref, gate_ref, ng_ref, w_ref, x_ref, mod_ref, out_ref, *,
                    gate_first, group, gate_idx):
    y = yf_ref[0].astype(F32) + yb_ref[0].astype(F32)
    gate = _silu(gate_ref[0].astype(F32))
    if gate_first:
        y = y * gate
    pieces = []
    for j in range(y.shape[1] // group):
        yg = y[:, j * group:(j + 1) * group]
        ms = jnp.mean(yg * yg, axis=-1, keepdims=True)
        pieces.append(yg * lax.rsqrt(ms + NORM_EPS))
    y = jnp.concatenate(pieces, axis=1)
    y = y * ng_ref[...] if gate_first else y * gate
    o = _dot(y.astype(BF16), w_ref[...])
    out_ref[0] = x_ref[0] + mod_ref[0][gate_idx:gate_idx + 1] * o


def _readout(y_f, y_b, gate, norm_g, w_out, x, mod, ctx_tiles, gate_first, group):
    n_batch, s_len, inner = y_f.shape
    d = x.shape[2]
    n_tiles = s_len // ROW_TILE
    row = lambda b, t: (b, t, 0)
    return pl.pallas_call(
        functools.partial(_readout_kernel, gate_first=gate_first, group=group, gate_idx=2),
        out_shape=jax.ShapeDtypeStruct(x.shape, F32),
        grid=(n_batch, n_tiles),
        in_specs=[
            pl.BlockSpec((1, ROW_TILE, inner), row),
            pl.BlockSpec((1, ROW_TILE, inner), row),
            pl.BlockSpec((1, ROW_TILE, inner), row),
            pl.BlockSpec((1, inner), lambda b, t: (0, 0)),
            pl.BlockSpec((inner, d), lambda b, t: (0, 0)),
            pl.BlockSpec((1, ROW_TILE, d), row),
            pl.BlockSpec((1, 6, d), lambda b, t: (jnp.where(t < ctx_tiles, n_batch, b), 0, 0)),
        ],
        out_specs=pl.BlockSpec((1, ROW_TILE, d), row),
        compiler_params=_cparams(("parallel", "arbitrary")),
    )(y_f, y_b, gate, norm_g.reshape(1, inner), w_out, x, mod)


def _router_kernel(x_ref, g_ref, mod_ref, wr_ref, br_ref, h_ref, idx_ref, gate_ref, rank_ref,
                   count_ref, base_ref):
    @pl.when(jnp.logical_and(pl.program_id(0) == 0, pl.program_id(1) == 0))
    def _():
        base_ref[...] = jnp.zeros_like(base_ref)

    h = _norm_modulate(x_ref[0], g_ref[...], mod_ref[0], 3, 4)
    _store_token_tiles(h_ref, (), h)
    h_hi, h_lo = _split_bf16(h)
    w_hi, w_lo = _split_bf16(wr_ref[...])
    logits = _dot(h_hi, w_hi) + _dot(h_lo, w_hi) + _dot(h_hi, w_lo) + br_ref[...]
    lane = lax.broadcasted_iota(jnp.int32, logits.shape, 1).astype(F32)
    work = logits
    idx_out = jnp.zeros(logits.shape, F32)
    val_out = jnp.full(logits.shape, NEG_BIG, F32)
    picks = []
    for k in range(TOP_K):
        m = jnp.max(work, axis=-1, keepdims=True)
        first = jnp.min(jnp.where(work == m, lane, float(LANES)), axis=-1, keepdims=True)
        pick = lane == first
        picks.append(pick)
        idx_out = jnp.where(lane == k, first, idx_out)
        val_out = jnp.where(lane == k, m, val_out)
        work = jnp.where(pick, -jnp.inf, work)
    top = jnp.max(val_out, axis=-1, keepdims=True)
    e = jnp.exp(val_out - top)
    idx_ref[0] = idx_out.astype(jnp.int32)
    gate_ref[0] = e / jnp.sum(e, axis=-1, keepdims=True)

    rows = logits.shape[0]
    onehot = jnp.zeros(logits.shape, F32)
    for pick in picks:
        onehot = onehot + jnp.where(pick, 1.0, 0.0)
    r = lax.broadcasted_iota(jnp.int32, (rows, rows), 0)
    c = lax.broadcasted_iota(jnp.int32, (rows, rows), 1)
    strict = jnp.where(c < r, 1.0, 0.0).astype(BF16)
    earlier = _dot(strict, onehot.astype(BF16)) + base_ref[...]
    rank_out = jnp.zeros(logits.shape, F32)
    for k, pick in enumerate(picks):
        rk = jnp.sum(jnp.where(pick, earlier, 0.0), axis=-1, keepdims=True)
        rank_out = jnp.where(lane == k, rk, rank_out)
    rank_ref[0] = rank_out.astype(jnp.int32)
    base_ref[...] = base_ref[...] + jnp.sum(onehot, axis=0, keepdims=True)
    count_ref[...] = base_ref[...].astype(jnp.int32)


def _router(x, g, mod, w_router, b_router, ctx_tiles):
    n_batch, s_len, d = x.shape
    n_exp = w_router.shape[1]
    wr = jnp.zeros((d, LANES), F32).at[:, :n_exp].set(w_router)
    br = jnp.full((1, LANES), NEG_BIG, F32).at[0, :n_exp].set(b_router)
    row = lambda b, t: (b, t, 0)
    n_tiles = s_len // ROW_TILE
    return pl.pallas_call(
        _router_kernel,
        out_shape=[jax.ShapeDtypeStruct((n_batch * s_len * TOKEN_SUBLANES, LANES), F32),
                   jax.ShapeDtypeStruct((n_batch, s_len, LANES), jnp.int32),
                   jax.ShapeDtypeStruct((n_batch, s_len, LANES), F32),
                   jax.ShapeDtypeStruct((n_batch, s_len, LANES), jnp.int32),
                   jax.ShapeDtypeStruct((1, LANES), jnp.int32)],
        grid=(n_batch, s_len // ROW_TILE),
        in_specs=[
            pl.BlockSpec((1, ROW_TILE, d), row),
            pl.BlockSpec((1, d), lambda b, t: (0, 0)),
            pl.BlockSpec((1, 6, d), lambda b, t: (jnp.where(t < ctx_tiles, n_batch, b), 0, 0)),
            pl.BlockSpec((d, LANES), lambda b, t: (0, 0)),
            pl.BlockSpec((1, LANES), lambda b, t: (0, 0)),
        ],
        out_specs=[pl.BlockSpec((ROW_TILE * TOKEN_SUBLANES, LANES), lambda b, t: (b * n_tiles + t, 0)),
                   pl.BlockSpec((1, ROW_TILE, LANES), row),
                   pl.BlockSpec((1, ROW_TILE, LANES), row),
                   pl.BlockSpec((1, ROW_TILE, LANES), row),
                   pl.BlockSpec((1, LANES), lambda b, t: (0, 0))],
        scratch_shapes=[pltpu.VMEM((1, LANES), F32)],
        compiler_params=_cparams(("arbitrary", "arbitrary")),
    )(x, g.reshape(1, d), mod, wr, br)


def _on_slot(parity_of, fn):
    for s in range(2):
        @pl.when(parity_of % 2 == s)
        def _(s=s):
            fn(s)


def _indexed_row_pipeline(i, n, idx_hbm, idx_smem, isem, issue_rows):
    def idx_copy(tile, slot):
        return pltpu.make_async_copy(idx_hbm.at[tile], idx_smem[slot], isem.at[slot])

    @pl.when(i == 0)
    def _():
        idx_copy(0, 0).start()
        idx_copy(0, 0).wait()
        issue_rows(0, 0)

        @pl.when(n > 1)
        def _():
            idx_copy(1, 1).start()

    @pl.when(i + 1 < n)
    def _():
        def nxt(s):
            idx_copy(i + 1, s).wait()
            issue_rows(i + 1, s)
        _on_slot(i + 1, nxt)

    @pl.when(i + 2 < n)
    def _():
        _on_slot(i + 2, lambda s: idx_copy(i + 2, s).start())


def _dispatch_kernel(zblk_ref, nused_ref, pos_hbm, h_ref, xs_hbm, zero_buf, hbuf, idx_a, idx_b, zsem,
                     gsem, isem):
    i = pl.program_id(0)
    n = pl.num_programs(0)
    n_rows = TOP_K * ROW_TILE
    block_rows = EXPERT_ROWS * TOKEN_SUBLANES

    @pl.when(i == 0)
    def _():
        zero_buf[...] = jnp.zeros_like(zero_buf)

        def zero_block(blk):
            first = pl.multiple_of(blk * block_rows, block_rows)
            cp = pltpu.make_async_copy(zero_buf, xs_hbm.at[pl.ds(first, block_rows), :], zsem)
            cp.start()
            cp.wait()

        for e in range(zblk_ref.shape[0]):
            @pl.when(zblk_ref[e] >= 0)
            def _(e=e):
                zero_block(zblk_ref[e])

        def tail(blk, carry):
            zero_block(blk)
            return carry
        lax.fori_loop(nused_ref[0], xs_hbm.shape[0] // block_rows, tail, 0)

    def idx_copy(tile, slot):
        return pltpu.make_async_copy(pos_hbm.at[tile], (idx_a, idx_b)[slot], isem.at[slot])

    def wait_rows(slot):
        span = pl.ds(0, n_rows * TOKEN_SUBLANES)
        pltpu.make_async_copy(xs_hbm.at[span, :], xs_hbm.at[span, :], gsem.at[slot]).wait()

    @pl.when(i == 0)
    def _():
        idx_copy(0, 0).start()

    def step(slot):
        idx = (idx_a, idx_b)[slot]
        idx_copy(i, slot).wait()

        @pl.when(i >= 2)
        def _():
            wait_rows(slot)

        hbuf[slot] = h_ref[...]

        def body(it, carry):
            for j in range(DMA_UNROLL):
                r = it * DMA_UNROLL + j
                src = pl.multiple_of((r // TOP_K) * TOKEN_SUBLANES, TOKEN_SUBLANES)
                dst = pl.multiple_of(idx[r], TOKEN_SUBLANES)
                pltpu.make_async_copy(hbuf.at[slot, pl.ds(src, TOKEN_SUBLANES), :],
                                      xs_hbm.at[pl.ds(dst, TOKEN_SUBLANES), :],
                                      gsem.at[slot]).start(priority=j % 2)
            return carry
        lax.fori_loop(0, n_rows // DMA_UNROLL, body, 0)
    _on_slot(i, step)

    @pl.when(i + 1 < n)
    def _():
        _on_slot(i + 1, lambda s: idx_copy(i + 1, s).start())

    @pl.when(i == n - 1)
    def _():
        @pl.when(n >= 2)
        def _():
            _on_slot(i + 1, wait_rows)
        _on_slot(i, wait_rows)


def _dispatch(pos_tiles, zero_blocks, n_used, h_tiles, cap):
    n_tiles = pos_tiles.shape[0]
    tile_rows = ROW_TILE * TOKEN_SUBLANES
    grid_spec = pltpu.PrefetchScalarGridSpec(
        num_scalar_prefetch=2,
        grid=(n_tiles,),
        in_specs=[pl.BlockSpec(memory_space=pl.ANY),
                  pl.BlockSpec((tile_rows, LANES), lambda i, zb, nu: (i, 0))],
        out_specs=pl.BlockSpec(memory_space=pl.ANY),
        scratch_shapes=[
            pltpu.VMEM((EXPERT_ROWS * TOKEN_SUBLANES, LANES), F32),
            pltpu.VMEM((2, tile_rows, LANES), F32),
            pltpu.SMEM((TOP_K * ROW_TILE,), jnp.int32),
            pltpu.SMEM((TOP_K * ROW_TILE,), jnp.int32),
            pltpu.SemaphoreType.DMA,
            pltpu.SemaphoreType.DMA((2,)),
            pltpu.SemaphoreType.DMA((2,)),
        ],
    )
    return pl.pallas_call(
        _dispatch_kernel,
        out_shape=jax.ShapeDtypeStruct((cap * TOKEN_SUBLANES, LANES), F32),
        grid_spec=grid_spec,
        compiler_params=_cparams(("arbitrary",)),
    )(zero_blocks, n_used, pos_tiles, h_tiles)


def _expert_kernel(bexp_ref, nused_ref, x_ref, w1_ref, b1_ref, w2_ref, b2_ref, out_ref, w1p, w2p):
    i = pl.program_id(0)
    prev = bexp_ref[jnp.maximum(i - 1, 0)]
    f2 = w1_ref.shape[2]

    @pl.when(jnp.logical_or(i == 0, bexp_ref[i] != prev))
    def _():
        src = lax.broadcasted_iota(jnp.int32, (MXU_DIM, MXU_DIM), 0)
        dst = lax.broadcasted_iota(jnp.int32, (MXU_DIM, MXU_DIM), 1)
        want = jnp.where(dst < LANES, 2 * dst, 2 * (dst - LANES) + 1)
        perm = jnp.where(src == want, 1.0, 0.0).astype(BF16)
        for c in range(f2 // MXU_DIM):
            blk = w1_ref[0, :, c * MXU_DIM:(c + 1) * MXU_DIM].astype(BF16)
            w1p[:, c * MXU_DIM:(c + 1) * MXU_DIM] = _dot(blk, perm).astype(BF16)
        w2p[...] = w2_ref[0].astype(BF16)

    @pl.when(i < nused_ref[0])
    def _():
        x = jnp.concatenate(_load_token_tiles(x_ref, (), 0, EXPERT_ROWS), axis=1).astype(BF16)
        hdn = _dot(x, w1p[...]) + b1_ref[0]
        acts = []
        for c in range(f2 // MXU_DIM):
            glu = jnp.minimum(hdn[:, c * MXU_DIM:c * MXU_DIM + LANES], SWIGLU_LIMIT)
            lin = jnp.clip(hdn[:, c * MXU_DIM + LANES:(c + 1) * MXU_DIM], -SWIGLU_LIMIT, SWIGLU_LIMIT)
            acts.append(glu * _sigmoid(SWIGLU_ALPHA * glu) * (lin + 1.0))
        act = jnp.concatenate(acts, axis=1).astype(BF16)
        _store_token_tiles(out_ref, (), _dot(act, w2p[...]) + b2_ref[0])

    @pl.when(i >= nused_ref[0])
    def _():
        out_ref[...] = jnp.zeros_like(out_ref)


def _experts(x_sorted, block_expert, n_used, w1, b1p, w2, b2):
    n_blocks = block_expert.shape[0]
    n_exp, d, f2 = w1.shape
    f = w2.shape[1]
    block_rows = EXPERT_ROWS * TOKEN_SUBLANES
    wspec = lambda shape: pl.BlockSpec(shape, lambda i, be, nu: (be[i], 0, 0))
    grid_spec = pltpu.PrefetchScalarGridSpec(
        num_scalar_prefetch=2,
        grid=(n_blocks,),
        in_specs=[
            pl.BlockSpec((block_rows, LANES), lambda i, be, nu: (jnp.minimum(i, nu[0] - 1), 0)),
            wspec((1, d, f2)), wspec((1, 1, f2)), wspec((1, f, d)), wspec((1, 1, d)),
        ],
        out_specs=pl.BlockSpec((block_rows, LANES), lambda i, be, nu: (i, 0)),
        scratch_shapes=[pltpu.VMEM((d, f2), BF16), pltpu.VMEM((f, d), BF16)],
    )
    return pl.pallas_call(
        _expert_kernel,
        out_shape=jax.ShapeDtypeStruct((n_blocks * block_rows, LANES), F32),
        grid_spec=grid_spec,
        compiler_params=_cparams(("arbitrary",)),
    )(block_expert, n_used, x_sorted, w1, b1p, w2, b2.reshape(n_exp, 1, d))


def _combine_kernel(pos_hbm, y_hbm, gate_ref, x_ref, mod_ref, out_ref, ybuf, idx_a, idx_b, gsem, isem):
    i = pl.program_id(0)
    n = pl.num_programs(0)
    n_rows = TOP_K * ROW_TILE

    def issue_rows(tile, slot):
        idx = (idx_a, idx_b)[slot]

        def body(it, carry):
            for j in range(DMA_UNROLL):
                r = it * DMA_UNROLL + j
                src = pl.multiple_of(idx[r], TOKEN_SUBLANES)
                dst = pl.multiple_of(r * TOKEN_SUBLANES, TOKEN_SUBLANES)
                pltpu.make_async_copy(y_hbm.at[pl.ds(src, TOKEN_SUBLANES), :],
                                      ybuf.at[slot, pl.ds(dst, TOKEN_SUBLANES), :],
                                      gsem.at[slot]).start(priority=j % 2)
            return carry
        lax.fori_loop(0, n_rows // DMA_UNROLL, body, 0)

    def wait_rows(slot):
        pltpu.make_async_copy(y_hbm.at[pl.ds(0, n_rows * TOKEN_SUBLANES), :], ybuf.at[slot],
                              gsem.at[slot]).wait()

    _indexed_row_pipeline(i, n, pos_hbm, (idx_a, idx_b), isem, issue_rows)
    _on_slot(i, wait_rows)

    def compute(slot):
        gate = gate_ref[...]
        acc = None
        for k in range(TOP_K):
            slabs = _load_token_tiles(ybuf, (slot,), k * ROW_TILE * TOKEN_SUBLANES, ROW_TILE)
            part = jnp.concatenate(slabs, axis=1) * gate[:, k:k + 1]
            acc = part if acc is None else acc + part
        out_ref[...] = x_ref[...] + mod_ref[0][5:6] * acc
    _on_slot(i, compute)


def _combine(pos_tiles, y_rows, gate_flat, x_flat, mod, n_batch, tiles_per_batch, ctx_tiles):
    n_tiles = pos_tiles.shape[0]
    d = x_flat.shape[1]

    def mod_map(i):
        b = i // tiles_per_batch
        return (jnp.where(i % tiles_per_batch < ctx_tiles, n_batch, b), 0, 0)

    return pl.pallas_call(
        _combine_kernel,
        out_shape=jax.ShapeDtypeStruct(x_flat.shape, F32),
        grid=(n_tiles,),
        in_specs=[
            pl.BlockSpec(memory_space=pl.ANY),
            pl.BlockSpec(memory_space=pl.ANY),
            pl.BlockSpec((ROW_TILE, LANES), lambda i: (i, 0)),
            pl.BlockSpec((ROW_TILE, d), lambda i: (i, 0)),
            pl.BlockSpec((1, 6, d), mod_map),
        ],
        out_specs=pl.BlockSpec((ROW_TILE, d), lambda i: (i, 0)),
        scratch_shapes=[
            pltpu.VMEM((2, TOP_K * ROW_TILE * TOKEN_SUBLANES, LANES), F32),
            pltpu.SMEM((TOP_K * ROW_TILE,), jnp.int32),
            pltpu.SMEM((TOP_K * ROW_TILE,), jnp.int32),
            pltpu.SemaphoreType.DMA((2,)),
            pltpu.SemaphoreType.DMA((2,)),
        ],
        compiler_params=_cparams(("arbitrary",)),
    )(pos_tiles, y_rows, gate_flat, x_flat, mod)


def _moe(x, g, mod, w_router, b_router, w1, b1p, w2, b2, ctx_tiles):
    n_batch, s_len, d = x.shape
    n_tok = n_batch * s_len
    n_exp = w1.shape[0]
    n_tiles = n_tok // ROW_TILE
    h, idx, gate, rank, count = _router(x, g, mod, w_router, b_router, ctx_tiles)

    counts = count[0, :n_exp]
    padded = (counts + EXPERT_ROWS - 1) // EXPERT_ROWS * EXPERT_ROWS
    pend = jnp.cumsum(padded)
    pstart = pend - padded
    n_blocks = -(-(n_tok * TOP_K) // EXPERT_ROWS) + n_exp
    block_start = jnp.arange(n_blocks, dtype=jnp.int32) * EXPERT_ROWS
    block_expert = jnp.minimum(jnp.sum(block_start[:, None] >= pend[None, :], axis=1),
                               n_exp - 1).astype(jnp.int32)
    n_used = (pend[-1] // EXPERT_ROWS).astype(jnp.int32).reshape(1)
    zero_blocks = jnp.where(padded > counts, pend // EXPERT_ROWS - 1, -1).astype(jnp.int32)

    top_i = idx[..., :TOP_K].reshape(n_tok, TOP_K)
    pos = ((pstart[top_i] + rank[..., :TOP_K].reshape(n_tok, TOP_K)) * TOKEN_SUBLANES).astype(jnp.int32)
    pos_token_major = pos.reshape(n_tiles, ROW_TILE * TOP_K)
    pos_slot_major = pos.reshape(n_tiles, ROW_TILE, TOP_K).transpose(0, 2, 1).reshape(n_tiles, -1)

    x_sorted = _dispatch(pos_token_major, zero_blocks, n_used, h, n_blocks * EXPERT_ROWS)
    y_rows = _experts(x_sorted, block_expert, n_used, w1, b1p, w2, b2)
    out = _combine(pos_slot_major, y_rows, gate.reshape(n_tok, LANES), x.reshape(n_tok, d), mod,
                   n_batch, s_len // ROW_TILE, ctx_tiles)
    return out.reshape(n_batch, s_len, d)


def _final_norm_kernel(x_ref, g_ref, out_ref):
    x = x_ref[0]
    ms = jnp.mean(x * x, axis=-1, keepdims=True)
    out_ref[0] = x * lax.rsqrt(ms + NORM_EPS) * g_ref[...]


def _final_norm(x, g, ctx_tiles, seq_len):
    n_batch, _, d = x.shape
    return pl.pallas_call(
        _final_norm_kernel,
        out_shape=jax.ShapeDtypeStruct((n_batch, seq_len, d), F32),
        grid=(n_batch, seq_len // ROW_TILE),
        in_specs=[pl.BlockSpec((1, ROW_TILE, d), lambda b, t: (b, t + ctx_tiles, 0)),
                  pl.BlockSpec((1, d), lambda b, t: (0, 0))],
        out_specs=pl.BlockSpec((1, ROW_TILE, d), lambda b, t: (b, t, 0)),
        compiler_params=_cparams(("parallel", "arbitrary")),
    )(x, g.reshape(1, d))


def _rope_tables(seq_len, ctx_len, half_dim):
    rows = seq_len // GRID_W
    t = jnp.arange(seq_len)
    row_pos = (t // GRID_W).astype(F32) - (rows - 1) / 2.0
    col_pos = (t % GRID_W).astype(F32) - (GRID_W - 1) / 2.0
    inv_freq = ROPE_BASE ** (-jnp.arange(0, half_dim, 2, dtype=F32) / half_dim)

    def tables(pos):
        ang = pos[:, None] * inv_freq[None, :]
        c, s = jnp.cos(ang), jnp.sin(ang)
        return jnp.concatenate([c, c], axis=1), jnp.concatenate([-s, s], axis=1)

    cr, sr = tables(row_pos)
    cc, sc = tables(col_pos)
    cos = jnp.concatenate([cr, cc], axis=1)
    sin = jnp.concatenate([sr, sc], axis=1)
    cos = jnp.concatenate([jnp.ones((ctx_len, cos.shape[1]), F32), cos], axis=0)
    sin = jnp.concatenate([jnp.zeros((ctx_len, sin.shape[1]), F32), sin], axis=0)
    return cos, sin


def kernel(x, c, ctx, c_ctx, w_mod, b_mod, norm1_g, norm2_g, ssd_w_in, ssd_conv_w, ssd_conv_b,
           ssd_dt_bias_f, ssd_dt_bias_b, ssd_a_log_f, ssd_a_log_b, ssd_d, ssd_norm_g, ssd_w_out,
           ret_w_in, ret_scale_f, ret_scale_b, ret_w_out, router_w, router_b,
           moe_w1, moe_b1, moe_w2, moe_b2, final_g):
    n_batch, seq_len, d = x.shape
    ctx_len = ctx.shape[1]
    depth = w_mod.shape[0]
    assert ctx_len % ROW_TILE == 0 and seq_len % ROW_TILE == 0
    ctx_tiles = ctx_len // ROW_TILE
    ctx_chunks = ctx_len // CHUNK

    n_heads = ssd_d.shape[1]
    ssd_inner = n_heads * SSD_HEAD_DIM
    bc_w = SSD_GROUPS * SSD_STATE
    conv_dim = ssd_inner + 2 * bc_w
    assert n_heads == LANES // 4 and ssd_inner % bc_w == 0

    ret_qk = (ret_w_in.shape[2] - 2 * ret_w_out.shape[1]) // 2
    ret_v = ret_w_out.shape[1]
    qk_dim = ret_qk // RET_HEADS
    assert qk_dim == 2 * LANES

    n_exp, _, f2 = moe_w1.shape[1:]
    assert f2 % MXU_DIM == 0 and d == TOKEN_SUBLANES * LANES

    cond = jnp.concatenate([c, c_ctx[None, :], jnp.zeros((16 - n_batch - 1, d), F32)], axis=0)
    mod_all = _modulation(cond, w_mod, b_mod).reshape(depth, 16, 6, d)

    xs = jnp.concatenate([ctx, x], axis=1)
    cos, sin = _rope_tables(seq_len, ctx_len, qk_dim // 2)

    for i in range(depth):
        j = i // 2
        mod = mod_all[i]
        if i % 2 == 0:
            w_in = ssd_w_in[j]
            w_dt = w_in[:, ssd_inner + conv_dim:].astype(BF16)
            weights = [w_in[:, :ssd_inner].astype(BF16),
                       w_in[:, ssd_inner:ssd_inner + conv_dim].astype(BF16),
                       jnp.concatenate([w_dt, w_dt], axis=1)]
            z, xbc, dt_raw = _in_proj(xs, norm1_g[i], mod, weights, [BF16, BF16, F32], ctx_tiles)
            dt_bias2 = jnp.concatenate([ssd_dt_bias_f[j], ssd_dt_bias_b[j]])
            dt_bias4 = jnp.concatenate([dt_bias2, dt_bias2]).reshape(1, LANES)
            a_log4 = jnp.concatenate([jnp.zeros((2 * n_heads,), F32), ssd_a_log_f[j],
                                      ssd_a_log_b[j]]).reshape(1, LANES)
            xbc_c, dtla = _ssd_conv(xbc, dt_raw, ssd_conv_w[j], ssd_conv_b[j], dt_bias4, a_log4,
                                    ctx_chunks)
            skip = jnp.repeat(ssd_d[j], SSD_HEAD_DIM).reshape(1, ssd_inner)
            y_f, y_b = _ssd_scan(xbc_c, dtla, skip, ssd_inner, ctx_chunks)
            xs = _readout(y_f, y_b, z, ssd_norm_g[j], ssd_w_out[j].astype(BF16), xs, mod,
                          ctx_tiles, True, ssd_inner // SSD_GROUPS)
        else:
            w_in = ret_w_in[j]
            bounds = [0, ret_qk, 2 * ret_qk, 2 * ret_qk + ret_v, 2 * ret_qk + 2 * ret_v]
            weights = [w_in[:, lo:hi].astype(BF16) for lo, hi in zip(bounds[:-1], bounds[1:])]
            q, k, v, gt = _in_proj(xs, norm1_g[i], mod, weights, [BF16] * 4, ctx_tiles,
                                   rope=(cos, sin, (0, 1)), k_scale=qk_dim ** -0.5)
            log_gamma = jnp.stack([jnp.log1p(-jnp.exp2(-ret_scale_f[j].astype(F32))),
                                   jnp.log1p(-jnp.exp2(-ret_scale_b[j].astype(F32)))])
            y_f, y_b = _ret_scan(q, k, v, log_gamma, ctx_chunks)
            xs = _readout(y_f, y_b, gt, jnp.ones((ret_v,), F32), ret_w_out[j].astype(BF16), xs, mod,
                          ctx_tiles, False, ret_v // RET_HEADS)
        b1p = moe_b1[i].reshape(n_exp, f2 // MXU_DIM, LANES, 2).transpose(0, 1, 3, 2).reshape(n_exp, 1, f2)
        xs = _moe(xs, norm2_g[i], mod, router_w[i], router_b[i], moe_w1[i], b1p, moe_w2[i], moe_b2[i],
                  ctx_tiles)
    return _final_norm(xs, final_g, ctx_tiles, seq_len)
```

```python
import functools

import jax
import jax.numpy as jnp
from jax import lax
from jax.experimental import pallas as pl
from jax.experimental.pallas import tpu as pltpu

F32 = jnp.float32
BF16 = jnp.bfloat16

NORM_EPS = 1e-6
CHUNK = 128
GRID_W = 64
ROPE_BASE = 10000.0

SSD_HEAD_DIM = 64
SSD_GROUPS = 4
SSD_STATE = 128
SSD_CONV = 5

RET_HEADS = 4

TOP_K = 4
SWIGLU_ALPHA = 1.702
SWIGLU_LIMIT = 7.0

LANES = 128
MXU_DIM = 256
ROW_TILE = 256
EXPERT_ROWS = 256
HALO = 16
TOKEN_SUBLANES = 8
DMA_UNROLL = 8
COMBINE_ROWS = 64
VMEM_LIMIT = 56 * 1024 * 1024
NEG_BIG = -1e30


def _cparams(sem):
    return pltpu.CompilerParams(dimension_semantics=sem, vmem_limit_bytes=VMEM_LIMIT)


def _split_bf16(a):
    hi = a.astype(BF16)
    lo = (a - hi.astype(F32)).astype(BF16)
    return hi, lo


def _dot(a, b):
    return jnp.dot(a, b, preferred_element_type=F32)


def _dot_tn(a, b):
    return lax.dot_general(a, b, (((0,), (0,)), ((), ())), preferred_element_type=F32)


def _dot_nt(a, b):
    return lax.dot_general(a, b, (((1,), (1,)), ((), ())), preferred_element_type=F32)


def _dot_split_lhs(a, b_bf16):
    hi, lo = _split_bf16(a)
    return _dot(hi, b_bf16) + _dot(lo, b_bf16)


def _store_token_tiles(ref, lead, value):
    rows = value.shape[0]
    for s in range(TOKEN_SUBLANES):
        ref[lead + (pl.ds(s, rows, stride=TOKEN_SUBLANES), slice(None))] = value[:, s * LANES:(s + 1) * LANES]


def _load_token_tiles(ref, lead, first_row, rows):
    return [ref[lead + (pl.ds(first_row + s, rows, stride=TOKEN_SUBLANES), slice(None))]
            for s in range(TOKEN_SUBLANES)]


def _sigmoid(x):
    return 1.0 / (1.0 + jnp.exp(-x))


def _silu(x):
    return x * _sigmoid(x)


def _mod_kernel(cond_ref, w_ref, b_ref, out_ref):
    s = _silu(cond_ref[...])
    s_hi, s_lo = _split_bf16(s)
    w_hi, w_lo = _split_bf16(w_ref[0])
    acc = _dot(s_hi, w_hi) + _dot(s_lo, w_hi) + _dot(s_hi, w_lo)
    out_ref[0] = acc + b_ref[0]


def _modulation(cond, w_mod, b_mod):
    depth, d, n = w_mod.shape
    rows = cond.shape[0]
    tn = 1024
    return pl.pallas_call(
        _mod_kernel,
        out_shape=jax.ShapeDtypeStruct((depth, rows, n), F32),
        grid=(depth, n // tn),
        in_specs=[
            pl.BlockSpec((rows, d), lambda l, j: (0, 0)),
            pl.BlockSpec((1, d, tn), lambda l, j: (l, 0, j)),
            pl.BlockSpec((1, 1, tn), lambda l, j: (l, 0, j)),
        ],
        out_specs=pl.BlockSpec((1, rows, tn), lambda l, j: (l, 0, j)),
        compiler_params=_cparams(("arbitrary", "arbitrary")),
    )(cond, w_mod, b_mod.reshape(depth, 1, n))


def _norm_modulate(x, g, mod, shift_idx, scale_idx):
    ms = jnp.mean(x * x, axis=-1, keepdims=True)
    h = x * lax.rsqrt(ms + NORM_EPS) * g
    return h * (1.0 + mod[scale_idx:scale_idx + 1]) + mod[shift_idx:shift_idx + 1]


def _in_proj_kernel(*refs, n_out, rope_outs, k_scale):
    x_ref, g_ref, mod_ref = refs[:3]
    w_refs = refs[3:3 + n_out]
    if rope_outs:
        cos_ref, sin_ref = refs[3 + n_out:5 + n_out]
        out_refs = refs[5 + n_out:]
    else:
        out_refs = refs[3 + n_out:]
    h = _norm_modulate(x_ref[0], g_ref[...], mod_ref[0], 0, 1).astype(BF16)
    for idx, w_ref in enumerate(w_refs):
        o = _dot(h, w_ref[...])
        if idx in rope_outs:
            cos = cos_ref[...]
            sin = sin_ref[...]
            pieces = []
            for j in range(o.shape[1] // LANES):
                u = o[:, j * LANES:(j + 1) * LANES]
                half = (j % 2) * LANES
                r = (u * cos[:, half:half + LANES]
                     + pltpu.roll(u, LANES // 2, axis=1) * sin[:, half:half + LANES])
                pieces.append(r)
            o = jnp.concatenate(pieces, axis=1)
            if idx == rope_outs[-1]:
                o = o * k_scale
        out_refs[idx][0] = o.astype(out_refs[idx].dtype)


def _in_proj(x, g, mod, weights, out_dtypes, ctx_tiles, rope=None, k_scale=1.0):
    n_batch, s_len, d = x.shape
    n_tiles = s_len // ROW_TILE
    in_specs = [
        pl.BlockSpec((1, ROW_TILE, d), lambda b, t: (b, t, 0)),
        pl.BlockSpec((1, d), lambda b, t: (0, 0)),
        pl.BlockSpec((1, 6, d), lambda b, t: (jnp.where(t < ctx_tiles, n_batch, b), 0, 0)),
    ] + [pl.BlockSpec(w.shape, lambda b, t: (0, 0)) for w in weights]
    args = [x, g.reshape(1, d), mod] + list(weights)
    splits = [(None, w.shape[1]) for w in weights]
    rope_outs = ()
    if rope is not None:
        cos, sin, rope_outs = rope
        in_specs += [pl.BlockSpec((ROW_TILE, cos.shape[1]), lambda b, t: (t, 0)),
                     pl.BlockSpec((ROW_TILE, sin.shape[1]), lambda b, t: (t, 0))]
        args += [cos, sin]
    out_shape = [jax.ShapeDtypeStruct((n_batch, s_len, wd), dt)
                 for (_, wd), dt in zip(splits, out_dtypes)]
    out_specs = [pl.BlockSpec((1, ROW_TILE, wd), lambda b, t: (b, t, 0)) for _, wd in splits]
    return pl.pallas_call(
        functools.partial(_in_proj_kernel, n_out=len(weights), rope_outs=tuple(rope_outs),
                          k_scale=k_scale),
        out_shape=out_shape,
        grid=(n_batch, n_tiles),
        in_specs=in_specs,
        out_specs=out_specs,
        compiler_params=_cparams(("parallel", "arbitrary")),
    )(*args)


def _ssd_conv_kernel(cur_ref, prev_ref, next_ref, dt_ref, cw_ref, cb_ref, dtb_ref, alog_ref,
                     xbc_ref, dtla_ref, ext_ref, *, ctx_chunks, n_chunks):
    c = pl.program_id(1)
    has_prev = jnp.logical_and(c != 0, c != ctx_chunks)
    has_next = jnp.logical_and(c != ctx_chunks - 1, c != n_chunks - 1)
    pad = SSD_CONV // 2
    ext_ref[0:8, :] = prev_ref[0].astype(F32)[HALO - 8:HALO, :] * jnp.where(has_prev, 1.0, 0.0)
    ext_ref[8:8 + CHUNK, :] = cur_ref[0].astype(F32)
    ext_ref[8 + CHUNK:16 + CHUNK, :] = next_ref[0].astype(F32)[0:8, :] * jnp.where(has_next, 1.0, 0.0)
    cw = cw_ref[...]
    acc = jnp.zeros((CHUNK, cur_ref.shape[2]), F32) + cb_ref[...]
    for k in range(SSD_CONV):
        acc = acc + ext_ref[8 - pad + k:8 - pad + k + CHUNK, :] * cw[k:k + 1, :]
    xbc_ref[0] = _silu(acc).astype(xbc_ref.dtype)
    raw = dt_ref[0] + dtb_ref[...]
    dt = jnp.maximum(raw, 0.0) + jnp.log(1.0 + jnp.exp(-jnp.abs(raw)))
    lane = lax.broadcasted_iota(jnp.int32, dt.shape, 1)
    a = -jnp.exp(alog_ref[...])
    dtla_ref[0] = jnp.where(lane < LANES // 2, dt, dt * a)


def _ssd_conv(xbc, dt_raw, conv_w, conv_b, dt_bias4, a_log4, ctx_chunks):
    n_batch, s_len, cdim = xbc.shape
    n_chunks = s_len // CHUNK
    per = CHUNK // HALO
    last_halo = s_len // HALO - 1
    return pl.pallas_call(
        functools.partial(_ssd_conv_kernel, ctx_chunks=ctx_chunks, n_chunks=n_chunks),
        out_shape=[jax.ShapeDtypeStruct((n_batch, s_len, cdim), BF16),
                   jax.ShapeDtypeStruct((n_batch, s_len, LANES), F32)],
        grid=(n_batch, n_chunks),
        in_specs=[
            pl.BlockSpec((1, CHUNK, cdim), lambda b, c: (b, c, 0)),
            pl.BlockSpec((1, HALO, cdim), lambda b, c: (b, jnp.maximum(c * per - 1, 0), 0)),
            pl.BlockSpec((1, HALO, cdim), lambda b, c: (b, jnp.minimum(c * per + per, last_halo), 0)),
            pl.BlockSpec((1, CHUNK, LANES), lambda b, c: (b, c, 0)),
            pl.BlockSpec((SSD_CONV, cdim), lambda b, c: (0, 0)),
            pl.BlockSpec((1, cdim), lambda b, c: (0, 0)),
            pl.BlockSpec((1, LANES), lambda b, c: (0, 0)),
            pl.BlockSpec((1, LANES), lambda b, c: (0, 0)),
        ],
        out_specs=[pl.BlockSpec((1, CHUNK, cdim), lambda b, c: (b, c, 0)),
                   pl.BlockSpec((1, CHUNK, LANES), lambda b, c: (b, c, 0))],
        scratch_shapes=[pltpu.VMEM((CHUNK + 16, cdim), F32)],
        compiler_params=_cparams(("parallel", "arbitrary")),
    )(xbc, xbc, xbc, dt_raw, conv_w, conv_b.reshape(1, cdim), dt_bias4, a_log4)


def _bwd_chunk(s, ctx_chunks, n_chunks):
    return jnp.where(s < ctx_chunks, ctx_chunks - 1 - s, n_chunks - 1 - (s - ctx_chunks))


def _ssd_chunk(xs_ref, b_ref, c_ref, dtla_ref, skip, state_ref, y_ref, *, reverse):
    n_heads = xs_ref.shape[2] // SSD_HEAD_DIM
    hp = xs_ref.shape[2]
    gw = hp // SSD_GROUPS
    off_dt = LANES // 4 if reverse else 0
    off_la = off_dt + LANES // 2

    row = lax.broadcasted_iota(jnp.int32, (CHUNK, CHUNK), 0)
    col = lax.broadcasted_iota(jnp.int32, (CHUNK, CHUNK), 1)
    before = (col >= row) if reverse else (col <= row)
    tri = jnp.where(before, 1.0, 0.0).astype(BF16)
    tri_t = jnp.where((row >= col) if reverse else (row <= col), 1.0, 0.0).astype(BF16)

    dtla = dtla_ref[0]
    hi, lo = _split_bf16(dtla)
    cum = _dot(tri, hi) + _dot(tri, lo)
    cum_t = _dot_tn(hi, tri_t) + _dot_tn(lo, tri_t)
    cum = jnp.where(lax.broadcasted_iota(jnp.int32, cum.shape, 1) >= LANES // 2, cum, 0.0)
    last = 0 if reverse else CHUNK - 1
    total = cum[last:last + 1, :]

    sec = lax.broadcasted_iota(jnp.int32, (LANES, hp), 0)
    head = lax.broadcasted_iota(jnp.int32, (LANES, hp), 1) // SSD_HEAD_DIM
    e_dt = jnp.where(sec == head + off_dt, 1.0, 0.0).astype(BF16)
    e_la = jnp.where(sec == head + off_la, 1.0, 0.0).astype(BF16)

    xs = xs_ref[0].astype(F32)
    lane_q = lax.broadcasted_iota(jnp.int32, (CHUNK, LANES), 1)
    is_dt = jnp.logical_and(lane_q >= off_dt, lane_q < off_dt + n_heads)
    is_la = jnp.logical_and(lane_q >= off_la, lane_q < off_la + n_heads)
    stacked = jnp.concatenate([jnp.where(is_dt, dtla, 0.0),
                               jnp.where(is_la, jnp.exp(cum), 0.0),
                               jnp.where(is_la, jnp.exp(total - cum), 0.0)], axis=0).astype(BF16)
    spread = _dot(stacked, e_dt + e_la)
    xdt = xs * spread[0:CHUNK]
    xdt_b = xdt.astype(BF16)
    dec_in = spread[CHUNK:2 * CHUNK]
    dec_end = spread[2 * CHUNK:3 * CHUNK]
    dec_all = _dot_split_lhs(jnp.exp(total), e_la)
    xdec_b = (xdt * dec_end).astype(BF16)

    cm = c_ref[0]
    bm = b_ref[0]
    lane = lax.broadcasted_iota(jnp.int32, (CHUNK, LANES), 1)
    low = lane < SSD_HEAD_DIM
    heads_per_group = n_heads // SSD_GROUPS
    for g in range(SSD_GROUPS):
        cg = cm[:, g * SSD_STATE:(g + 1) * SSD_STATE]
        bg = bm[:, g * SSD_STATE:(g + 1) * SSD_STATE]
        scores = _dot_nt(cg, bg)
        cols = slice(g * gw, (g + 1) * gw)
        h_enter = state_ref[:, cols]
        y_inter = _dot(cg, h_enter.astype(BF16)) * dec_in[:, cols]
        state_ref[:, cols] = h_enter * dec_all[:, cols] + _dot_tn(bg, xdec_b[:, cols])
        for pair in range(heads_per_group // 2):
            base = g * gw + pair * LANES
            rhs = xdt_b[:, base:base + LANES]
            ms = []
            for k in range(2):
                h = g * heads_per_group + pair * 2 + k
                seg = cum[:, off_la + h:off_la + h + 1] - cum_t[off_la + h:off_la + h + 1, :]
                ms.append((scores * jnp.exp(jnp.where(before, seg, NEG_BIG))).astype(BF16))
            zero = jnp.zeros_like(rhs)
            rhs2 = jnp.concatenate([jnp.where(low, rhs, zero), jnp.where(low, zero, rhs)], axis=0)
            y = _dot(jnp.concatenate(ms, axis=1), rhs2) + y_inter[:, pair * LANES:(pair + 1) * LANES]
            if skip is not None:
                y = y + xs[:, base:base + LANES] * skip[:, base:base + LANES]
            y_ref[0, :, base:base + LANES] = y.astype(y_ref.dtype)


def _ssd_scan_kernel(xs_f, b_f, c_f, dtla_f, xs_b, b_b, c_b, dtla_b, skip_ref,
                     yf_ref, yb_ref, st_f, st_b):
    @pl.when(pl.program_id(1) == 0)
    def _():
        st_f[...] = jnp.zeros_like(st_f)
        st_b[...] = jnp.zeros_like(st_b)

    _ssd_chunk(xs_f, b_f, c_f, dtla_f, skip_ref[...], st_f, yf_ref, reverse=False)
    _ssd_chunk(xs_b, b_b, c_b, dtla_b, None, st_b, yb_ref, reverse=True)


def _ssd_scan(xbc, dtla, skip, hp, ctx_chunks):
    n_batch, s_len, _ = xbc.shape
    n_chunks = s_len // CHUNK
    bc_w = SSD_GROUPS * SSD_STATE
    b_blk = hp // bc_w
    fwd = lambda s: s
    bwd = lambda s: _bwd_chunk(s, ctx_chunks, n_chunks)

    def specs(order):
        return [
            pl.BlockSpec((1, CHUNK, hp), lambda b, s: (b, order(s), 0)),
            pl.BlockSpec((1, CHUNK, bc_w), lambda b, s: (b, order(s), b_blk)),
            pl.BlockSpec((1, CHUNK, bc_w), lambda b, s: (b, order(s), b_blk + 1)),
            pl.BlockSpec((1, CHUNK, LANES), lambda b, s: (b, order(s), 0)),
        ]

    return pl.pallas_call(
        _ssd_scan_kernel,
        out_shape=[jax.ShapeDtypeStruct((n_batch, s_len, hp), BF16)] * 2,
        grid=(n_batch, n_chunks),
        in_specs=specs(fwd) + specs(bwd) + [pl.BlockSpec((1, hp), lambda b, s: (0, 0))],
        out_specs=[pl.BlockSpec((1, CHUNK, hp), lambda b, s: (b, fwd(s), 0)),
                   pl.BlockSpec((1, CHUNK, hp), lambda b, s: (b, bwd(s), 0))],
        scratch_shapes=[pltpu.VMEM((SSD_STATE, hp), F32), pltpu.VMEM((SSD_STATE, hp), F32)],
        compiler_params=_cparams(("parallel", "arbitrary")),
    )(xbc, xbc, xbc, dtla, xbc, xbc, xbc, dtla, skip)


def _ret_chunk(q_ref, k_ref, v_ref, lg_ref, state_ref, y_ref, *, reverse):
    qk = q_ref.shape[2] // RET_HEADS
    vd = v_ref.shape[2] // RET_HEADS
    row = lax.broadcasted_iota(jnp.int32, (CHUNK, CHUNK), 0)
    col = lax.broadcasted_iota(jnp.int32, (CHUNK, CHUNK), 1)
    before = (col >= row) if reverse else (col <= row)
    dist = jnp.abs(row - col).astype(F32)
    n_in = ((CHUNK - row) if reverse else (row + 1)).astype(F32)
    n_end = ((row) if reverse else (CHUNK - 1 - row)).astype(F32)
    q = q_ref[0]
    k = k_ref[0]
    v = v_ref[0]
    d_sel = 1 if reverse else 0
    for h in range(RET_HEADS):
        lg = lg_ref[d_sel, h]
        qh = q[:, h * qk:(h + 1) * qk]
        kh = k[:, h * qk:(h + 1) * qk]
        vh = v[:, h * vd:(h + 1) * vd]
        scores = _dot_nt(qh, kh)
        m = (scores * jnp.exp(jnp.where(before, dist * lg, NEG_BIG))).astype(BF16)
        y_intra = _dot(m, vh)
        h_enter = state_ref[h]
        r_in = jnp.exp(n_in * lg)
        y_inter = _dot(qh, h_enter.astype(BF16))
        scale = jnp.concatenate([r_in] * (vd // CHUNK), axis=1)
        y_ref[0, :, h * vd:(h + 1) * vd] = (y_intra + y_inter * scale).astype(y_ref.dtype)
        r_end = jnp.exp(n_end * lg)
        k_scale = jnp.concatenate([r_end] * (qk // CHUNK), axis=1)
        kdec = (kh.astype(F32) * k_scale).astype(BF16)
        state_ref[h] = h_enter * jnp.exp(lg * CHUNK) + _dot_tn(kdec, vh)


def _ret_scan_kernel(lg_ref, q_f, k_f, v_f, q_b, k_b, v_b, yf_ref, yb_ref, st_f, st_b):
    @pl.when(pl.program_id(1) == 0)
    def _():
        st_f[...] = jnp.zeros_like(st_f)
        st_b[...] = jnp.zeros_like(st_b)

    _ret_chunk(q_f, k_f, v_f, lg_ref, st_f, yf_ref, reverse=False)
    _ret_chunk(q_b, k_b, v_b, lg_ref, st_b, yb_ref, reverse=True)


def _ret_scan(q, k, v, log_gamma, ctx_chunks):
    n_batch, s_len, qk_tot = q.shape
    v_tot = v.shape[2]
    n_chunks = s_len // CHUNK
    fwd = lambda s: s
    bwd = lambda s: _bwd_chunk(s, ctx_chunks, n_chunks)

    def specs(order):
        return [
            pl.BlockSpec((1, CHUNK, qk_tot), lambda b, s: (b, order(s), 0)),
            pl.BlockSpec((1, CHUNK, qk_tot), lambda b, s: (b, order(s), 0)),
            pl.BlockSpec((1, CHUNK, v_tot), lambda b, s: (b, order(s), 0)),
        ]

    return pl.pallas_call(
        _ret_scan_kernel,
        out_shape=[jax.ShapeDtypeStruct((n_batch, s_len, v_tot), BF16)] * 2,
        grid=(n_batch, n_chunks),
        in_specs=[pl.BlockSpec(memory_space=pltpu.SMEM)] + specs(fwd) + specs(bwd),
        out_specs=[pl.BlockSpec((1, CHUNK, v_tot), lambda b, s: (b, fwd(s), 0)),
                   pl.BlockSpec((1, CHUNK, v_tot), lambda b, s: (b, bwd(s), 0))],
        scratch_shapes=[pltpu.VMEM((RET_HEADS, qk_tot // RET_HEADS, v_tot // RET_HEADS), F32)] * 2,
        compiler_params=_cparams(("parallel", "arbitrary")),
    )(log_gamma, q, k, v, q, k, v)


def _readout_kernel(yf_ref, yb_ref, gate_ref, ng_ref, w_ref, x_ref, mod_ref, out_ref, *,
                    gate_first, group, gate_idx):
    y = yf_ref[0].astype(F32) + yb_ref[0].astype(F32)
    gate = _silu(gate_ref[0].astype(F32))
    if gate_first:
        y = y * gate
    pieces = []
    for j in range(y.shape[1] // group):
        yg = y[:, j * group:(j + 1) * group]
        ms = jnp.mean(yg * yg, axis=-1, keepdims=True)
        pieces.append(yg * lax.rsqrt(ms + NORM_EPS))
    y = jnp.concatenate(pieces, axis=1)
    y = y * ng_ref[...] if gate_first else y * gate
    o = _dot(y.astype(BF16), w_ref[...])
    out_ref[0] = x_ref[0] + mod_ref[0][gate_idx:gate_idx + 1] * o


def _readout(y_f, y_b, gate, norm_g, w_out, x, mod, ctx_tiles, gate_first, group):
    n_batch, s_len, inner = y_f.shape
    d = x.shape[2]
    n_tiles = s_len // ROW_TILE
    row = lambda b, t: (b, t, 0)
    return pl.pallas_call(
        functools.partial(_readout_kernel, gate_first=gate_first, group=group, gate_idx=2),
        out_shape=jax.ShapeDtypeStruct(x.shape, F32),
        grid=(n_batch, n_tiles),
        in_specs=[
            pl.BlockSpec((1, ROW_TILE, inner), row),
            pl.BlockSpec((1, ROW_TILE, inner), row),
            pl.BlockSpec((1, ROW_TILE, inner), row),
            pl.BlockSpec((1, inner), lambda b, t: (0, 0)),
            pl.BlockSpec((inner, d), lambda b, t: (0, 0)),
            pl.BlockSpec((1, ROW_TILE, d), row),
            pl.BlockSpec((1, 6, d), lambda b, t: (jnp.where(t < ctx_tiles, n_batch, b), 0, 0)),
        ],
        out_specs=pl.BlockSpec((1, ROW_TILE, d), row),
        compiler_params=_cparams(("parallel", "arbitrary")),
    )(y_f, y_b, gate, norm_g.reshape(1, inner), w_out, x, mod)


def _router_kernel(x_ref, g_ref, mod_ref, wr_ref, br_ref, h_ref, idx_ref, gate_ref, rank_ref,
                   count_ref, base_ref):
    @pl.when(jnp.logical_and(pl.program_id(0) == 0, pl.program_id(1) == 0))
    def _():
        base_ref[...] = jnp.zeros_like(base_ref)

    h = _norm_modulate(x_ref[0], g_ref[...], mod_ref[0], 3, 4)
    _store_token_tiles(h_ref, (), h)
    h_hi, h_lo = _split_bf16(h)
    w_hi, w_lo = _split_bf16(wr_ref[...])
    logits = _dot(h_hi, w_hi) + _dot(h_lo, w_hi) + _dot(h_hi, w_lo) + br_ref[...]
    lane = lax.broadcasted_iota(jnp.int32, logits.shape, 1).astype(F32)
    work = logits
    idx_out = jnp.zeros(logits.shape, F32)
    val_out = jnp.full(logits.shape, NEG_BIG, F32)
    picks = []
    for k in range(TOP_K):
        m = jnp.max(work, axis=-1, keepdims=True)
        first = jnp.min(jnp.where(work == m, lane, float(LANES)), axis=-1, keepdims=True)
        pick = lane == first
        picks.append(pick)
        idx_out = jnp.where(lane == k, first, idx_out)
        val_out = jnp.where(lane == k, m, val_out)
        work = jnp.where(pick, -jnp.inf, work)
    top = jnp.max(val_out, axis=-1, keepdims=True)
    e = jnp.exp(val_out - top)
    idx_ref[0] = idx_out.astype(jnp.int32)
    gate_ref[0] = e / jnp.sum(e, axis=-1, keepdims=True)

    rows = logits.shape[0]
    onehot = jnp.zeros(logits.shape, F32)
    for pick in picks:
        onehot = onehot + jnp.where(pick, 1.0, 0.0)
    r = lax.broadcasted_iota(jnp.int32, (rows, rows), 0)
    c = lax.broadcasted_iota(jnp.int32, (rows, rows), 1)
    strict = jnp.where(c < r, 1.0, 0.0).astype(BF16)
    earlier = _dot(strict, onehot.astype(BF16)) + base_ref[...]
    rank_out = jnp.zeros(logits.shape, F32)
    for k, pick in enumerate(picks):
        rk = jnp.sum(jnp.where(pick, earlier, 0.0), axis=-1, keepdims=True)
        rank_out = jnp.where(lane == k, rk, rank_out)
    rank_ref[0] = rank_out.astype(jnp.int32)
    base_ref[...] = base_ref[...] + jnp.sum(onehot, axis=0, keepdims=True)
    count_ref[...] = base_ref[...].astype(jnp.int32)


def _router(x, g, mod, w_router, b_router, ctx_tiles):
    n_batch, s_len, d = x.shape
    n_exp = w_router.shape[1]
    wr = jnp.zeros((d, LANES), F32).at[:, :n_exp].set(w_router)
    br = jnp.full((1, LANES), NEG_BIG, F32).at[0, :n_exp].set(b_router)
    row = lambda b, t: (b, t, 0)
    n_tiles = s_len // ROW_TILE
    return pl.pallas_call(
        _router_kernel,
        out_shape=[jax.ShapeDtypeStruct((n_batch * s_len * TOKEN_SUBLANES, LANES), F32),
                   jax.ShapeDtypeStruct((n_batch, s_len, LANES), jnp.int32),
                   jax.ShapeDtypeStruct((n_batch, s_len, LANES), F32),
                   jax.ShapeDtypeStruct((n_batch, s_len, LANES), jnp.int32),
                   jax.ShapeDtypeStruct((1, LANES), jnp.int32)],
        grid=(n_batch, s_len // ROW_TILE),
        in_specs=[
            pl.BlockSpec((1, ROW_TILE, d), row),
            pl.BlockSpec((1, d), lambda b, t: (0, 0)),
            pl.BlockSpec((1, 6, d), lambda b, t: (jnp.where(t < ctx_tiles, n_batch, b), 0, 0)),
            pl.BlockSpec((d, LANES), lambda b, t: (0, 0)),
            pl.BlockSpec((1, LANES), lambda b, t: (0, 0)),
        ],
        out_specs=[pl.BlockSpec((ROW_TILE * TOKEN_SUBLANES, LANES), lambda b, t: (b * n_tiles + t, 0)),
                   pl.BlockSpec((1, ROW_TILE, LANES), row),
                   pl.BlockSpec((1, ROW_TILE, LANES), row),
                   pl.BlockSpec((1, ROW_TILE, LANES), row),
                   pl.BlockSpec((1, LANES), lambda b, t: (0, 0))],
        scratch_shapes=[pltpu.VMEM((1, LANES), F32)],
        compiler_params=_cparams(("arbitrary", "arbitrary")),
    )(x, g.reshape(1, d), mod, wr, br)


def _on_slot(parity_of, fn):
    for s in range(2):
        @pl.when(parity_of % 2 == s)
        def _(s=s):
            fn(s)


def _indexed_row_pipeline(i, n, idx_hbm, idx_smem, isem, issue_rows):
    def idx_copy(tile, slot):
        return pltpu.make_async_copy(idx_hbm.at[tile], idx_smem[slot], isem.at[slot])

    @pl.when(i == 0)
    def _():
        idx_copy(0, 0).start()
        idx_copy(0, 0).wait()
        issue_rows(0, 0)

        @pl.when(n > 1)
        def _():
            idx_copy(1, 1).start()

    @pl.when(i + 1 < n)
    def _():
        def nxt(s):
            idx_copy(i + 1, s).wait()
            issue_rows(i + 1, s)
        _on_slot(i + 1, nxt)

    @pl.when(i + 2 < n)
    def _():
        _on_slot(i + 2, lambda s: idx_copy(i + 2, s).start())


def _dispatch_kernel(zblk_ref, nused_ref, pos_hbm, h_ref, xs_hbm, zero_buf, hbuf, idx_a, idx_b, zsem,
                     gsem, isem):
    i = pl.program_id(0)
    n = pl.num_programs(0)
    n_rows = TOP_K * ROW_TILE
    block_rows = EXPERT_ROWS * TOKEN_SUBLANES

    @pl.when(i == 0)
    def _():
        zero_buf[...] = jnp.zeros_like(zero_buf)

        def zero_block(blk):
            first = pl.multiple_of(blk * block_rows, block_rows)
            cp = pltpu.make_async_copy(zero_buf, xs_hbm.at[pl.ds(first, block_rows), :], zsem)
            cp.start()
            cp.wait()

        for e in range(zblk_ref.shape[0]):
            @pl.when(zblk_ref[e] >= 0)
            def _(e=e):
                zero_block(zblk_ref[e])

        def tail(blk, carry):
            zero_block(blk)
            return carry
        lax.fori_loop(nused_ref[0], xs_hbm.shape[0] // block_rows, tail, 0)

    def idx_copy(tile, slot):
        return pltpu.make_async_copy(pos_hbm.at[tile], (idx_a, idx_b)[slot], isem.at[slot])

    def wait_rows(slot):
        span = pl.ds(0, n_rows * TOKEN_SUBLANES)
        pltpu.make_async_copy(xs_hbm.at[span, :], xs_hbm.at[span, :], gsem.at[slot]).wait()

    @pl.when(i == 0)
    def _():
        idx_copy(0, 0).start()

    def step(slot):
        idx = (idx_a, idx_b)[slot]
        idx_copy(i, slot).wait()

        @pl.when(i >= 2)
        def _():
            wait_rows(slot)

        hbuf[slot] = h_ref[...]

        def body(it, carry):
            for j in range(DMA_UNROLL):
                r = it * DMA_UNROLL + j
                tok = it * (DMA_UNROLL // TOP_K) + j // TOP_K
                src = pl.multiple_of(tok * TOKEN_SUBLANES, TOKEN_SUBLANES)
                dst = pl.multiple_of(idx[r], TOKEN_SUBLANES)
                pltpu.make_async_copy(hbuf.at[slot, pl.ds(src, TOKEN_SUBLANES), :],
                                      xs_hbm.at[pl.ds(dst, TOKEN_SUBLANES), :],
                                      gsem.at[slot]).start(priority=j % 2)
            return carry
        lax.fori_loop(0, n_rows // DMA_UNROLL, body, 0)
    _on_slot(i, step)

    @pl.when(i + 1 < n)
    def _():
        _on_slot(i + 1, lambda s: idx_copy(i + 1, s).start())

    @pl.when(i == n - 1)
    def _():
        @pl.when(n >= 2)
        def _():
            _on_slot(i + 1, wait_rows)
        _on_slot(i, wait_rows)


def _dispatch(pos_tiles, zero_blocks, n_used, h_tiles, cap):
    n_tiles = pos_tiles.shape[0]
    tile_rows = ROW_TILE * TOKEN_SUBLANES
    grid_spec = pltpu.PrefetchScalarGridSpec(
        num_scalar_prefetch=2,
        grid=(n_tiles,),
        in_specs=[pl.BlockSpec(memory_space=pl.ANY),
                  pl.BlockSpec((tile_rows, LANES), lambda i, zb, nu: (i, 0))],
        out_specs=pl.BlockSpec(memory_space=pl.ANY),
        scratch_shapes=[
            pltpu.VMEM((EXPERT_ROWS * TOKEN_SUBLANES, LANES), F32),
            pltpu.VMEM((2, tile_rows, LANES), F32),
            pltpu.SMEM((TOP_K * ROW_TILE,), jnp.int32),
            pltpu.SMEM((TOP_K * ROW_TILE,), jnp.int32),
            pltpu.SemaphoreType.DMA,
            pltpu.SemaphoreType.DMA((2,)),
            pltpu.SemaphoreType.DMA((2,)),
        ],
    )
    return pl.pallas_call(
        _dispatch_kernel,
        out_shape=jax.ShapeDtypeStruct((cap * TOKEN_SUBLANES, LANES), F32),
        grid_spec=grid_spec,
        compiler_params=_cparams(("arbitrary",)),
    )(zero_blocks, n_used, pos_tiles, h_tiles)


def _expert_kernel(bexp_ref, nused_ref, x_ref, w1_ref, b1_ref, w2_ref, b2_ref, out_ref, w1p, w2p):
    i = pl.program_id(0)
    prev = bexp_ref[jnp.maximum(i - 1, 0)]
    f2 = w1_ref.shape[3]

    @pl.when(jnp.logical_or(i == 0, bexp_ref[i] != prev))
    def _():
        src = lax.broadcasted_iota(jnp.int32, (MXU_DIM, MXU_DIM), 0)
        dst = lax.broadcasted_iota(jnp.int32, (MXU_DIM, MXU_DIM), 1)
        want = jnp.where(dst < LANES, 2 * dst, 2 * (dst - LANES) + 1)
        perm = jnp.where(src == want, 1.0, 0.0).astype(BF16)
        for c in range(f2 // MXU_DIM):
            blk = w1_ref[0, 0, :, c * MXU_DIM:(c + 1) * MXU_DIM].astype(BF16)
            w1p[:, c * MXU_DIM:(c + 1) * MXU_DIM] = _dot(blk, perm).astype(BF16)
        w2p[...] = w2_ref[0, 0].astype(BF16)

    @pl.when(i < nused_ref[0])
    def _():
        x = jnp.concatenate(_load_token_tiles(x_ref, (), 0, EXPERT_ROWS), axis=1).astype(BF16)
        hdn = _dot(x, w1p[...]) + b1_ref[0]
        acts = []
        for c in range(f2 // MXU_DIM):
            glu = jnp.minimum(hdn[:, c * MXU_DIM:c * MXU_DIM + LANES], SWIGLU_LIMIT)
            lin = jnp.clip(hdn[:, c * MXU_DIM + LANES:(c + 1) * MXU_DIM], -SWIGLU_LIMIT, SWIGLU_LIMIT)
            acts.append(glu * _sigmoid(SWIGLU_ALPHA * glu) * (lin + 1.0))
        act = jnp.concatenate(acts, axis=1).astype(BF16)
        _store_token_tiles(out_ref, (), _dot(act, w2p[...]) + b2_ref[0])

    @pl.when(i >= nused_ref[0])
    def _():
        out_ref[...] = jnp.zeros_like(out_ref)


def _experts(x_sorted, block_expert, n_used, layer, w1_all, b1p, w2_all, b2):
    n_blocks = block_expert.shape[0]
    _, n_exp, d, f2 = w1_all.shape
    f = w2_all.shape[2]
    block_rows = EXPERT_ROWS * TOKEN_SUBLANES
    wspec = lambda shape: pl.BlockSpec(shape, lambda i, be, nu: (be[i], 0, 0))
    wspec_all = lambda shape: pl.BlockSpec(shape, lambda i, be, nu: (layer, be[i], 0, 0))
    grid_spec = pltpu.PrefetchScalarGridSpec(
        num_scalar_prefetch=2,
        grid=(n_blocks,),
        in_specs=[
            pl.BlockSpec((block_rows, LANES), lambda i, be, nu: (jnp.minimum(i, nu[0] - 1), 0)),
            wspec_all((1, 1, d, f2)), wspec((1, 1, f2)), wspec_all((1, 1, f, d)), wspec((1, 1, d)),
        ],
        out_specs=pl.BlockSpec((block_rows, LANES), lambda i, be, nu: (i, 0)),
        scratch_shapes=[pltpu.VMEM((d, f2), BF16), pltpu.VMEM((f, d), BF16)],
    )
    return pl.pallas_call(
        _expert_kernel,
        out_shape=jax.ShapeDtypeStruct((n_blocks * block_rows, LANES), F32),
        grid_spec=grid_spec,
        compiler_params=_cparams(("arbitrary",)),
    )(block_expert, n_used, x_sorted, w1_all, b1p, w2_all, b2.reshape(n_exp, 1, d))


def _combine_kernel(pos_hbm, y_hbm, gate_ref, x_ref, mod_ref, out_ref, ybuf, idx_a, idx_b, gsem, isem):
    i = pl.program_id(0)
    n = pl.num_programs(0)
    n_rows = TOP_K * ROW_TILE

    def issue_rows(tile, slot):
        idx = (idx_a, idx_b)[slot]

        def body(it, carry):
            for j in range(DMA_UNROLL):
                r = it * DMA_UNROLL + j
                src = pl.multiple_of(idx[r], TOKEN_SUBLANES)
                dst = pl.multiple_of(r * TOKEN_SUBLANES, TOKEN_SUBLANES)
                pltpu.make_async_copy(y_hbm.at[pl.ds(src, TOKEN_SUBLANES), :],
                                      ybuf.at[slot, pl.ds(dst, TOKEN_SUBLANES), :],
                                      gsem.at[slot]).start(priority=j % 2)
            return carry
        lax.fori_loop(0, n_rows // DMA_UNROLL, body, 0)

    def wait_rows(slot):
        pltpu.make_async_copy(y_hbm.at[pl.ds(0, n_rows * TOKEN_SUBLANES), :], ybuf.at[slot],
                              gsem.at[slot]).wait()

    _indexed_row_pipeline(i, n, pos_hbm, (idx_a, idx_b), isem, issue_rows)
    _on_slot(i, wait_rows)

    def compute(slot):
        g2 = mod_ref[0][5:6]
        sub = COMBINE_ROWS
        for rb in range(ROW_TILE // sub):
            rows = slice(rb * sub, (rb + 1) * sub)
            gate = gate_ref[rows, :]
            gk = [jnp.broadcast_to(gate[:, k:k + 1], (sub, LANES)) for k in range(TOP_K)]
            for s in range(TOKEN_SUBLANES):
                acc = None
                for k in range(TOP_K):
                    first = (k * ROW_TILE + rb * sub) * TOKEN_SUBLANES + s
                    part = ybuf[slot, pl.ds(first, sub, stride=TOKEN_SUBLANES), :] * gk[k]
                    acc = part if acc is None else acc + part
                cols = slice(s * LANES, (s + 1) * LANES)
                out_ref[rows, cols] = x_ref[rows, cols] + g2[:, cols] * acc
    _on_slot(i, compute)


def _combine(pos_tiles, y_rows, gate_flat, x_flat, mod, n_batch, tiles_per_batch, ctx_tiles):
    n_tiles = pos_tiles.shape[0]
    d = x_flat.shape[1]

    def mod_map(i):
        b = i // tiles_per_batch
        return (jnp.where(i % tiles_per_batch < ctx_tiles, n_batch, b), 0, 0)

    return pl.pallas_call(
        _combine_kernel,
        out_shape=jax.ShapeDtypeStruct(x_flat.shape, F32),
        grid=(n_tiles,),
        in_specs=[
            pl.BlockSpec(memory_space=pl.ANY),
            pl.BlockSpec(memory_space=pl.ANY),
            pl.BlockSpec((ROW_TILE, LANES), lambda i: (i, 0)),
            pl.BlockSpec((ROW_TILE, d), lambda i: (i, 0)),
            pl.BlockSpec((1, 6, d), mod_map),
        ],
        out_specs=pl.BlockSpec((ROW_TILE, d), lambda i: (i, 0)),
        scratch_shapes=[
            pltpu.VMEM((2, TOP_K * ROW_TILE * TOKEN_SUBLANES, LANES), F32),
            pltpu.SMEM((TOP_K * ROW_TILE,), jnp.int32),
            pltpu.SMEM((TOP_K * ROW_TILE,), jnp.int32),
            pltpu.SemaphoreType.DMA((2,)),
            pltpu.SemaphoreType.DMA((2,)),
        ],
        compiler_params=_cparams(("arbitrary",)),
    )(pos_tiles, y_rows, gate_flat, x_flat, mod)


def _moe(x, g, mod, w_router, b_router, layer, w1_all, b1p, w2_all, b2, ctx_tiles):
    n_batch, s_len, d = x.shape
    n_tok = n_batch * s_len
    n_exp = w1_all.shape[1]
    n_tiles = n_tok // ROW_TILE
    h, idx, gate, rank, count = _router(x, g, mod, w_router, b_router, ctx_tiles)

    counts = count[0, :n_exp]
    padded = (counts + EXPERT_ROWS - 1) // EXPERT_ROWS * EXPERT_ROWS
    pend = jnp.cumsum(padded)
    pstart = pend - padded
    n_blocks = -(-(n_tok * TOP_K) // EXPERT_ROWS) + n_exp
    block_start = jnp.arange(n_blocks, dtype=jnp.int32) * EXPERT_ROWS
    block_expert = jnp.minimum(jnp.sum(block_start[:, None] >= pend[None, :], axis=1),
                               n_exp - 1).astype(jnp.int32)
    n_used = (pend[-1] // EXPERT_ROWS).astype(jnp.int32).reshape(1)
    zero_blocks = jnp.where(padded > counts, pend // EXPERT_ROWS - 1, -1).astype(jnp.int32)

    top_i = idx[..., :TOP_K].reshape(n_tok, TOP_K)
    pos = ((pstart[top_i] + rank[..., :TOP_K].reshape(n_tok, TOP_K)) * TOKEN_SUBLANES).astype(jnp.int32)
    pos_token_major = pos.reshape(n_tiles, ROW_TILE * TOP_K)
    pos_slot_major = pos.reshape(n_tiles, ROW_TILE, TOP_K).transpose(0, 2, 1).reshape(n_tiles, -1)

    x_sorted = _dispatch(pos_token_major, zero_blocks, n_used, h, n_blocks * EXPERT_ROWS)
    y_rows = _experts(x_sorted, block_expert, n_used, layer, w1_all, b1p, w2_all, b2)
    out = _combine(pos_slot_major, y_rows, gate.reshape(n_tok, LANES), x.reshape(n_tok, d), mod,
                   n_batch, s_len // ROW_TILE, ctx_tiles)
    return out.reshape(n_batch, s_len, d)


def _final_norm_kernel(x_ref, g_ref, out_ref):
    x = x_ref[0]
    ms = jnp.mean(x * x, axis=-1, keepdims=True)
    out_ref[0] = x * lax.rsqrt(ms + NORM_EPS) * g_ref[...]


def _final_norm(x, g, ctx_tiles, seq_len):
    n_batch, _, d = x.shape
    return pl.pallas_call(
        _final_norm_kernel,
        out_shape=jax.ShapeDtypeStruct((n_batch, seq_len, d), F32),
        grid=(n_batch, seq_len // ROW_TILE),
        in_specs=[pl.BlockSpec((1, ROW_TILE, d), lambda b, t: (b, t + ctx_tiles, 0)),
                  pl.BlockSpec((1, d), lambda b, t: (0, 0))],
        out_specs=pl.BlockSpec((1, ROW_TILE, d), lambda b, t: (b, t, 0)),
        compiler_params=_cparams(("parallel", "arbitrary")),
    )(x, g.reshape(1, d))


def _rope_tables(seq_len, ctx_len, half_dim):
    rows = seq_len // GRID_W
    t = jnp.arange(seq_len)
    row_pos = (t // GRID_W).astype(F32) - (rows - 1) / 2.0
    col_pos = (t % GRID_W).astype(F32) - (GRID_W - 1) / 2.0
    inv_freq = ROPE_BASE ** (-jnp.arange(0, half_dim, 2, dtype=F32) / half_dim)

    def tables(pos):
        ang = pos[:, None] * inv_freq[None, :]
        c, s = jnp.cos(ang), jnp.sin(ang)
        return jnp.concatenate([c, c], axis=1), jnp.concatenate([-s, s], axis=1)

    cr, sr = tables(row_pos)
    cc, sc = tables(col_pos)
    cos = jnp.concatenate([cr, cc], axis=1)
    sin = jnp.concatenate([sr, sc], axis=1)
    cos = jnp.concatenate([jnp.ones((ctx_len, cos.shape[1]), F32), cos], axis=0)
    sin = jnp.concatenate([jnp.zeros((ctx_len, sin.shape[1]), F32), sin], axis=0)
    return cos, sin


def kernel(x, c, ctx, c_ctx, w_mod, b_mod, norm1_g, norm2_g, ssd_w_in, ssd_conv_w, ssd_conv_b,
           ssd_dt_bias_f, ssd_dt_bias_b, ssd_a_log_f, ssd_a_log_b, ssd_d, ssd_norm_g, ssd_w_out,
           ret_w_in, ret_scale_f, ret_scale_b, ret_w_out, router_w, router_b,
           moe_w1, moe_b1, moe_w2, moe_b2, final_g):
    n_batch, seq_len, d = x.shape
    ctx_len = ctx.shape[1]
    depth = w_mod.shape[0]
    assert ctx_len % ROW_TILE == 0 and seq_len % ROW_TILE == 0
    ctx_tiles = ctx_len // ROW_TILE
    ctx_chunks = ctx_len // CHUNK

    n_heads = ssd_d.shape[1]
    ssd_inner = n_heads * SSD_HEAD_DIM
    bc_w = SSD_GROUPS * SSD_STATE
    conv_dim = ssd_inner + 2 * bc_w
    assert n_heads == LANES // 4 and ssd_inner % bc_w == 0

    ret_qk = (ret_w_in.shape[2] - 2 * ret_w_out.shape[1]) // 2
    ret_v = ret_w_out.shape[1]
    qk_dim = ret_qk // RET_HEADS
    assert qk_dim == 2 * LANES

    n_exp, _, f2 = moe_w1.shape[1:]
    assert f2 % MXU_DIM == 0 and d == TOKEN_SUBLANES * LANES

    cond = jnp.concatenate([c, c_ctx[None, :], jnp.zeros((16 - n_batch - 1, d), F32)], axis=0)
    mod_all = _modulation(cond, w_mod, b_mod).reshape(depth, 16, 6, d)

    xs = jnp.concatenate([ctx, x], axis=1)
    cos, sin = _rope_tables(seq_len, ctx_len, qk_dim // 2)

    for i in range(depth):
        j = i // 2
        mod = mod_all[i]
        if i % 2 == 0:
            w_in = ssd_w_in[j]
            w_dt = w_in[:, ssd_inner + conv_dim:].astype(BF16)
            weights = [w_in[:, :ssd_inner].astype(BF16),
                       w_in[:, ssd_inner:ssd_inner + conv_dim].astype(BF16),
                       jnp.concatenate([w_dt, w_dt], axis=1)]
            z, xbc, dt_raw = _in_proj(xs, norm1_g[i], mod, weights, [BF16, BF16, F32], ctx_tiles)
            dt_bias2 = jnp.concatenate([ssd_dt_bias_f[j], ssd_dt_bias_b[j]])
            dt_bias4 = jnp.concatenate([dt_bias2, dt_bias2]).reshape(1, LANES)
            a_log4 = jnp.concatenate([jnp.zeros((2 * n_heads,), F32), ssd_a_log_f[j],
                                      ssd_a_log_b[j]]).reshape(1, LANES)
            xbc_c, dtla = _ssd_conv(xbc, dt_raw, ssd_conv_w[j], ssd_conv_b[j], dt_bias4, a_log4,
                                    ctx_chunks)
            skip = jnp.repeat(ssd_d[j], SSD_HEAD_DIM).reshape(1, ssd_inner)
            y_f, y_b = _ssd_scan(xbc_c, dtla, skip, ssd_inner, ctx_chunks)
            xs = _readout(y_f, y_b, z, ssd_norm_g[j], ssd_w_out[j].astype(BF16), xs, mod,
                          ctx_tiles, True, ssd_inner // SSD_GROUPS)
        else:
            w_in = ret_w_in[j]
            bounds = [0, ret_qk, 2 * ret_qk, 2 * ret_qk + ret_v, 2 * ret_qk + 2 * ret_v]
            weights = [w_in[:, lo:hi].astype(BF16) for lo, hi in zip(bounds[:-1], bounds[1:])]
            q, k, v, gt = _in_proj(xs, norm1_g[i], mod, weights, [BF16] * 4, ctx_tiles,
                                   rope=(cos, sin, (0, 1)), k_scale=qk_dim ** -0.5)
            log_gamma = jnp.stack([jnp.log1p(-jnp.exp2(-ret_scale_f[j].astype(F32))),
                                   jnp.log1p(-jnp.exp2(-ret_scale_b[j].astype(F32)))])
            y_f, y_b = _ret_scan(q, k, v, log_gamma, ctx_chunks)
            xs = _readout(y_f, y_b, gt, jnp.ones((ret_v,), F32), ret_w_out[j].astype(BF16), xs, mod,
                          ctx_tiles, False, ret_v // RET_HEADS)
        b1p = moe_b1[i].reshape(n_exp, f2 // MXU_DIM, LANES, 2).transpose(0, 1, 3, 2).reshape(n_exp, 1, f2)
        xs = _moe(xs, norm2_g[i], mod, router_w[i], router_b[i], i, moe_w1, b1p, moe_w2, moe_b2[i],
                  ctx_tiles)
    return _final_norm(xs, final_g, ctx_tiles, seq_len)
```

```python
import functools

import jax
import jax.numpy as jnp
from jax import lax
from jax.experimental import pallas as pl
from jax.experimental.pallas import tpu as pltpu

F32 = jnp.float32
BF16 = jnp.bfloat16

NORM_EPS = 1e-6
CHUNK = 128
GRID_W = 64
ROPE_BASE = 10000.0

SSD_HEAD_DIM = 64
SSD_GROUPS = 4
SSD_STATE = 128
SSD_CONV = 5

RET_HEADS = 4

TOP_K = 4
SWIGLU_ALPHA = 1.702
SWIGLU_LIMIT = 7.0

LANES = 128
MXU_DIM = 256
ROW_TILE = 256
EXPERT_ROWS = 512
HALO = 16
TOKEN_SUBLANES = 8
DMA_UNROLL = 8
COMBINE_ROWS = 64
VMEM_LIMIT = 56 * 1024 * 1024
NEG_BIG = -1e30


def _cparams(sem):
    return pltpu.CompilerParams(dimension_semantics=sem, vmem_limit_bytes=VMEM_LIMIT)


def _split_bf16(a):
    hi = a.astype(BF16)
    lo = (a - hi.astype(F32)).astype(BF16)
    return hi, lo


def _dot(a, b):
    return jnp.dot(a, b, preferred_element_type=F32)


def _dot_tn(a, b):
    return lax.dot_general(a, b, (((0,), (0,)), ((), ())), preferred_element_type=F32)


def _dot_nt(a, b):
    return lax.dot_general(a, b, (((1,), (1,)), ((), ())), preferred_element_type=F32)


def _dot_split_lhs(a, b_bf16):
    hi, lo = _split_bf16(a)
    return _dot(hi, b_bf16) + _dot(lo, b_bf16)


def _store_token_tiles(ref, lead, value):
    rows = value.shape[0]
    for s in range(TOKEN_SUBLANES):
        ref[lead + (pl.ds(s, rows, stride=TOKEN_SUBLANES), slice(None))] = value[:, s * LANES:(s + 1) * LANES]


def _load_token_tiles(ref, lead, first_row, rows):
    return [ref[lead + (pl.ds(first_row + s, rows, stride=TOKEN_SUBLANES), slice(None))]
            for s in range(TOKEN_SUBLANES)]


def _sigmoid(x):
    return 1.0 / (1.0 + jnp.exp(-x))


def _silu(x):
    return x * _sigmoid(x)


def _mod_kernel(cond_ref, w_ref, b_ref, out_ref):
    s = _silu(cond_ref[...])
    s_hi, s_lo = _split_bf16(s)
    w_hi, w_lo = _split_bf16(w_ref[0])
    acc = _dot(s_hi, w_hi) + _dot(s_lo, w_hi) + _dot(s_hi, w_lo)
    out_ref[0] = acc + b_ref[0]


def _modulation(cond, w_mod, b_mod):
    depth, d, n = w_mod.shape
    rows = cond.shape[0]
    tn = 1024
    return pl.pallas_call(
        _mod_kernel,
        out_shape=jax.ShapeDtypeStruct((depth, rows, n), F32),
        grid=(depth, n // tn),
        in_specs=[
            pl.BlockSpec((rows, d), lambda l, j: (0, 0)),
            pl.BlockSpec((1, d, tn), lambda l, j: (l, 0, j)),
            pl.BlockSpec((1, 1, tn), lambda l, j: (l, 0, j)),
        ],
        out_specs=pl.BlockSpec((1, rows, tn), lambda l, j: (l, 0, j)),
        compiler_params=_cparams(("arbitrary", "arbitrary")),
    )(cond, w_mod, b_mod.reshape(depth, 1, n))


def _norm_modulate(x, g, mod, shift_idx, scale_idx):
    ms = jnp.mean(x * x, axis=-1, keepdims=True)
    h = x * lax.rsqrt(ms + NORM_EPS) * g
    return h * (1.0 + mod[scale_idx:scale_idx + 1]) + mod[shift_idx:shift_idx + 1]


def _in_proj_kernel(*refs, n_out, rope_outs, k_scale):
    x_ref, g_ref, mod_ref = refs[:3]
    w_refs = refs[3:3 + n_out]
    if rope_outs:
        cos_ref, sin_ref = refs[3 + n_out:5 + n_out]
        out_refs = refs[5 + n_out:]
    else:
        out_refs = refs[3 + n_out:]
    h = _norm_modulate(x_ref[0], g_ref[...], mod_ref[0], 0, 1).astype(BF16)
    for idx, w_ref in enumerate(w_refs):
        o = _dot(h, w_ref[...])
        if idx in rope_outs:
            cos = cos_ref[...]
            sin = sin_ref[...]
            pieces = []
            for j in range(o.shape[1] // LANES):
                u = o[:, j * LANES:(j + 1) * LANES]
                half = (j % 2) * LANES
                r = (u * cos[:, half:half + LANES]
                     + pltpu.roll(u, LANES // 2, axis=1) * sin[:, half:half + LANES])
                pieces.append(r)
            o = jnp.concatenate(pieces, axis=1)
            if idx == rope_outs[-1]:
                o = o * k_scale
        out_refs[idx][0] = o.astype(out_refs[idx].dtype)


def _in_proj(x, g, mod, weights, out_dtypes, ctx_tiles, rope=None, k_scale=1.0):
    n_batch, s_len, d = x.shape
    n_tiles = s_len // ROW_TILE
    in_specs = [
        pl.BlockSpec((1, ROW_TILE, d), lambda b, t: (b, t, 0)),
        pl.BlockSpec((1, d), lambda b, t: (0, 0)),
        pl.BlockSpec((1, 6, d), lambda b, t: (jnp.where(t < ctx_tiles, n_batch, b), 0, 0)),
    ] + [pl.BlockSpec(w.shape, lambda b, t: (0, 0)) for w in weights]
    args = [x, g.reshape(1, d), mod] + list(weights)
    splits = [(None, w.shape[1]) for w in weights]
    rope_outs = ()
    if rope is not None:
        cos, sin, rope_outs = rope
        in_specs += [pl.BlockSpec((ROW_TILE, cos.shape[1]), lambda b, t: (t, 0)),
                     pl.BlockSpec((ROW_TILE, sin.shape[1]), lambda b, t: (t, 0))]
        args += [cos, sin]
    out_shape = [jax.ShapeDtypeStruct((n_batch, s_len, wd), dt)
                 for (_, wd), dt in zip(splits, out_dtypes)]
    out_specs = [pl.BlockSpec((1, ROW_TILE, wd), lambda b, t: (b, t, 0)) for _, wd in splits]
    return pl.pallas_call(
        functools.partial(_in_proj_kernel, n_out=len(weights), rope_outs=tuple(rope_outs),
                          k_scale=k_scale),
        out_shape=out_shape,
        grid=(n_batch, n_tiles),
        in_specs=in_specs,
        out_specs=out_specs,
        compiler_params=_cparams(("parallel", "arbitrary")),
    )(*args)


def _ssd_conv_kernel(cur_ref, prev_ref, next_ref, dt_ref, cw_ref, cb_ref, dtb_ref, alog_ref,
                     xbc_ref, dtla_ref, ext_ref, *, ctx_chunks, n_chunks):
    c = pl.program_id(1)
    has_prev = jnp.logical_and(c != 0, c != ctx_chunks)
    has_next = jnp.logical_and(c != ctx_chunks - 1, c != n_chunks - 1)
    pad = SSD_CONV // 2
    ext_ref[0:8, :] = prev_ref[0].astype(F32)[HALO - 8:HALO, :] * jnp.where(has_prev, 1.0, 0.0)
    ext_ref[8:8 + CHUNK, :] = cur_ref[0].astype(F32)
    ext_ref[8 + CHUNK:16 + CHUNK, :] = next_ref[0].astype(F32)[0:8, :] * jnp.where(has_next, 1.0, 0.0)
    cw = cw_ref[...]
    ext_rows = CHUNK + 16
    taps = [k for k in range(SSD_CONV) if k != pad]
    sel_r = lax.broadcasted_iota(jnp.int32, (len(taps) * CHUNK, ext_rows), 0)
    sel_c = lax.broadcasted_iota(jnp.int32, (len(taps) * CHUNK, ext_rows), 1)
    want = jnp.zeros_like(sel_r)
    for n, k in enumerate(taps):
        in_tap = jnp.logical_and(sel_r >= n * CHUNK, sel_r < (n + 1) * CHUNK)
        want = jnp.where(in_tap, sel_r - n * CHUNK + 8 - pad + k, want)
    select = jnp.where(sel_c == want, 1.0, 0.0).astype(BF16)
    shifted = _dot(select, ext_ref[...].astype(BF16))
    acc = ext_ref[8:8 + CHUNK, :] * cw[pad:pad + 1, :] + cb_ref[...]
    for n, k in enumerate(taps):
        acc = acc + shifted[n * CHUNK:(n + 1) * CHUNK] * cw[k:k + 1, :]
    xbc_ref[0] = _silu(acc).astype(xbc_ref.dtype)
    raw = dt_ref[0] + dtb_ref[...]
    dt = jnp.maximum(raw, 0.0) + jnp.log(1.0 + jnp.exp(-jnp.abs(raw)))
    lane = lax.broadcasted_iota(jnp.int32, dt.shape, 1)
    a = -jnp.exp(alog_ref[...])
    dtla_ref[0] = jnp.where(lane < LANES // 2, dt, dt * a)


def _ssd_conv(xbc, dt_raw, conv_w, conv_b, dt_bias4, a_log4, ctx_chunks):
    n_batch, s_len, cdim = xbc.shape
    n_chunks = s_len // CHUNK
    per = CHUNK // HALO
    last_halo = s_len // HALO - 1
    return pl.pallas_call(
        functools.partial(_ssd_conv_kernel, ctx_chunks=ctx_chunks, n_chunks=n_chunks),
        out_shape=[jax.ShapeDtypeStruct((n_batch, s_len, cdim), BF16),
                   jax.ShapeDtypeStruct((n_batch, s_len, LANES), F32)],
        grid=(n_batch, n_chunks),
        in_specs=[
            pl.BlockSpec((1, CHUNK, cdim), lambda b, c: (b, c, 0)),
            pl.BlockSpec((1, HALO, cdim), lambda b, c: (b, jnp.maximum(c * per - 1, 0), 0)),
            pl.BlockSpec((1, HALO, cdim), lambda b, c: (b, jnp.minimum(c * per + per, last_halo), 0)),
            pl.BlockSpec((1, CHUNK, LANES), lambda b, c: (b, c, 0)),
            pl.BlockSpec((SSD_CONV, cdim), lambda b, c: (0, 0)),
            pl.BlockSpec((1, cdim), lambda b, c: (0, 0)),
            pl.BlockSpec((1, LANES), lambda b, c: (0, 0)),
            pl.BlockSpec((1, LANES), lambda b, c: (0, 0)),
        ],
        out_specs=[pl.BlockSpec((1, CHUNK, cdim), lambda b, c: (b, c, 0)),
                   pl.BlockSpec((1, CHUNK, LANES), lambda b, c: (b, c, 0))],
        scratch_shapes=[pltpu.VMEM((CHUNK + 16, cdim), F32)],
        compiler_params=_cparams(("parallel", "arbitrary")),
    )(xbc, xbc, xbc, dt_raw, conv_w, conv_b.reshape(1, cdim), dt_bias4, a_log4)


def _bwd_chunk(s, ctx_chunks, n_chunks):
    return jnp.where(s < ctx_chunks, ctx_chunks - 1 - s, n_chunks - 1 - (s - ctx_chunks))


def _ssd_chunk(xs_ref, b_ref, c_ref, dtla_ref, skip, state_ref, y_ref, *, reverse):
    n_heads = xs_ref.shape[2] // SSD_HEAD_DIM
    hp = xs_ref.shape[2]
    gw = hp // SSD_GROUPS
    off_dt = LANES // 4 if reverse else 0
    off_la = off_dt + LANES // 2

    row = lax.broadcasted_iota(jnp.int32, (CHUNK, CHUNK), 0)
    col = lax.broadcasted_iota(jnp.int32, (CHUNK, CHUNK), 1)
    before = (col >= row) if reverse else (col <= row)
    tri = jnp.where(before, 1.0, 0.0).astype(BF16)
    tri_t = jnp.where((row >= col) if reverse else (row <= col), 1.0, 0.0).astype(BF16)

    dtla = dtla_ref[0]
    hi, lo = _split_bf16(dtla)
    cum = _dot(tri, hi) + _dot(tri, lo)
    cum_t = _dot_tn(hi, tri_t) + _dot_tn(lo, tri_t)
    cum = jnp.where(lax.broadcasted_iota(jnp.int32, cum.shape, 1) >= LANES // 2, cum, 0.0)
    last = 0 if reverse else CHUNK - 1
    total = cum[last:last + 1, :]

    sec = lax.broadcasted_iota(jnp.int32, (LANES, hp), 0)
    head = lax.broadcasted_iota(jnp.int32, (LANES, hp), 1) // SSD_HEAD_DIM
    e_dt = jnp.where(sec == head + off_dt, 1.0, 0.0).astype(BF16)
    e_la = jnp.where(sec == head + off_la, 1.0, 0.0).astype(BF16)

    xs = xs_ref[0].astype(F32)
    lane_q = lax.broadcasted_iota(jnp.int32, (CHUNK, LANES), 1)
    is_dt = jnp.logical_and(lane_q >= off_dt, lane_q < off_dt + n_heads)
    is_la = jnp.logical_and(lane_q >= off_la, lane_q < off_la + n_heads)
    stacked = jnp.concatenate([jnp.where(is_dt, dtla, 0.0),
                               jnp.where(is_la, jnp.exp(cum), 0.0),
                               jnp.where(is_la, jnp.exp(total - cum), 0.0)], axis=0).astype(BF16)
    spread = _dot(stacked, e_dt + e_la)
    xdt = xs * spread[0:CHUNK]
    xdt_b = xdt.astype(BF16)
    dec_in = spread[CHUNK:2 * CHUNK]
    dec_end = spread[2 * CHUNK:3 * CHUNK]
    dec_all = _dot_split_lhs(jnp.exp(total), e_la)
    xdec_b = (xdt * dec_end).astype(BF16)

    cm = c_ref[0]
    bm = b_ref[0]
    lane = lax.broadcasted_iota(jnp.int32, (CHUNK, LANES), 1)
    low = lane < SSD_HEAD_DIM
    heads_per_group = n_heads // SSD_GROUPS
    for g in range(SSD_GROUPS):
        cg = cm[:, g * SSD_STATE:(g + 1) * SSD_STATE]
        bg = bm[:, g * SSD_STATE:(g + 1) * SSD_STATE]
        scores = _dot_nt(cg, bg)
        cols = slice(g * gw, (g + 1) * gw)
        h_enter = state_ref[:, cols]
        y_inter = _dot(cg, h_enter.astype(BF16)) * dec_in[:, cols]
        state_ref[:, cols] = h_enter * dec_all[:, cols] + _dot_tn(bg, xdec_b[:, cols])
        for pair in range(heads_per_group // 2):
            base = g * gw + pair * LANES
            rhs = xdt_b[:, base:base + LANES]
            ms = []
            for k in range(2):
                h = g * heads_per_group + pair * 2 + k
                seg = cum[:, off_la + h:off_la + h + 1] - cum_t[off_la + h:off_la + h + 1, :]
                ms.append((scores * jnp.exp(jnp.where(before, seg, NEG_BIG))).astype(BF16))
            zero = jnp.zeros_like(rhs)
            rhs2 = jnp.concatenate([jnp.where(low, rhs, zero), jnp.where(low, zero, rhs)], axis=0)
            y = _dot(jnp.concatenate(ms, axis=1), rhs2) + y_inter[:, pair * LANES:(pair + 1) * LANES]
            if skip is not None:
                y = y + xs[:, base:base + LANES] * skip[:, base:base + LANES]
            y_ref[0, :, base:base + LANES] = y.astype(y_ref.dtype)


def _ssd_scan_kernel(xs_f, b_f, c_f, dtla_f, xs_b, b_b, c_b, dtla_b, skip_ref,
                     yf_ref, yb_ref, st_f, st_b):
    @pl.when(pl.program_id(1) == 0)
    def _():
        st_f[...] = jnp.zeros_like(st_f)
        st_b[...] = jnp.zeros_like(st_b)

    _ssd_chunk(xs_f, b_f, c_f, dtla_f, skip_ref[...], st_f, yf_ref, reverse=False)
    _ssd_chunk(xs_b, b_b, c_b, dtla_b, None, st_b, yb_ref, reverse=True)


def _ssd_scan(xbc, dtla, skip, hp, ctx_chunks):
    n_batch, s_len, _ = xbc.shape
    n_chunks = s_len // CHUNK
    bc_w = SSD_GROUPS * SSD_STATE
    b_blk = hp // bc_w
    fwd = lambda s: s
    bwd = lambda s: _bwd_chunk(s, ctx_chunks, n_chunks)

    def specs(order):
        return [
            pl.BlockSpec((1, CHUNK, hp), lambda b, s: (b, order(s), 0)),
            pl.BlockSpec((1, CHUNK, bc_w), lambda b, s: (b, order(s), b_blk)),
            pl.BlockSpec((1, CHUNK, bc_w), lambda b, s: (b, order(s), b_blk + 1)),
            pl.BlockSpec((1, CHUNK, LANES), lambda b, s: (b, order(s), 0)),
        ]

    return pl.pallas_call(
        _ssd_scan_kernel,
        out_shape=[jax.ShapeDtypeStruct((n_batch, s_len, hp), BF16)] * 2,
        grid=(n_batch, n_chunks),
        in_specs=specs(fwd) + specs(bwd) + [pl.BlockSpec((1, hp), lambda b, s: (0, 0))],
        out_specs=[pl.BlockSpec((1, CHUNK, hp), lambda b, s: (b, fwd(s), 0)),
                   pl.BlockSpec((1, CHUNK, hp), lambda b, s: (b, bwd(s), 0))],
        scratch_shapes=[pltpu.VMEM((SSD_STATE, hp), F32), pltpu.VMEM((SSD_STATE, hp), F32)],
        compiler_params=_cparams(("parallel", "arbitrary")),
    )(xbc, xbc, xbc, dtla, xbc, xbc, xbc, dtla, skip)


def _ret_chunk(q_ref, k_ref, v_ref, lg_ref, state_ref, y_ref, *, reverse):
    qk = q_ref.shape[2] // RET_HEADS
    vd = v_ref.shape[2] // RET_HEADS
    row = lax.broadcasted_iota(jnp.int32, (CHUNK, CHUNK), 0)
    col = lax.broadcasted_iota(jnp.int32, (CHUNK, CHUNK), 1)
    before = (col >= row) if reverse else (col <= row)
    dist = jnp.abs(row - col).astype(F32)
    n_in = ((CHUNK - row) if reverse else (row + 1)).astype(F32)
    n_end = ((row) if reverse else (CHUNK - 1 - row)).astype(F32)
    q = q_ref[0]
    k = k_ref[0]
    v = v_ref[0]
    d_sel = 1 if reverse else 0
    for h in range(RET_HEADS):
        lg = lg_ref[d_sel, h]
        qh = q[:, h * qk:(h + 1) * qk]
        kh = k[:, h * qk:(h + 1) * qk]
        vh = v[:, h * vd:(h + 1) * vd]
        scores = _dot_nt(qh, kh)
        m = (scores * jnp.exp(jnp.where(before, dist * lg, NEG_BIG))).astype(BF16)
        y_intra = _dot(m, vh)
        h_enter = state_ref[h]
        r_in = jnp.exp(n_in * lg)
        y_inter = _dot(qh, h_enter.astype(BF16))
        scale = jnp.concatenate([r_in] * (vd // CHUNK), axis=1)
        y_ref[0, :, h * vd:(h + 1) * vd] = (y_intra + y_inter * scale).astype(y_ref.dtype)
        r_end = jnp.exp(n_end * lg)
        k_scale = jnp.concatenate([r_end] * (qk // CHUNK), axis=1)
        kdec = (kh.astype(F32) * k_scale).astype(BF16)
        state_ref[h] = h_enter * jnp.exp(lg * CHUNK) + _dot_tn(kdec, vh)


def _ret_scan_kernel(lg_ref, q_f, k_f, v_f, q_b, k_b, v_b, yf_ref, yb_ref, st_f, st_b):
    @pl.when(pl.program_id(1) == 0)
    def _():
        st_f[...] = jnp.zeros_like(st_f)
        st_b[...] = jnp.zeros_like(st_b)

    _ret_chunk(q_f, k_f, v_f, lg_ref, st_f, yf_ref, reverse=False)
    _ret_chunk(q_b, k_b, v_b, lg_ref, st_b, yb_ref, reverse=True)


def _ret_scan(q, k, v, log_gamma, ctx_chunks):
    n_batch, s_len, qk_tot = q.shape
    v_tot = v.shape[2]
    n_chunks = s_len // CHUNK
    fwd = lambda s: s
    bwd = lambda s: _bwd_chunk(s, ctx_chunks, n_chunks)

    def specs(order):
        return [
            pl.BlockSpec((1, CHUNK, qk_tot), lambda b, s: (b, order(s), 0)),
            pl.BlockSpec((1, CHUNK, qk_tot), lambda b, s: (b, order(s), 0)),
            pl.BlockSpec((1, CHUNK, v_tot), lambda b, s: (b, order(s), 0)),
        ]

    return pl.pallas_call(
        _ret_scan_kernel,
        out_shape=[jax.ShapeDtypeStruct((n_batch, s_len, v_tot), BF16)] * 2,
        grid=(n_batch, n_chunks),
        in_specs=[pl.BlockSpec(memory_space=pltpu.SMEM)] + specs(fwd) + specs(bwd),
        out_specs=[pl.BlockSpec((1, CHUNK, v_tot), lambda b, s: (b, fwd(s), 0)),
                   pl.BlockSpec((1, CHUNK, v_tot), lambda b, s: (b, bwd(s), 0))],
        scratch_shapes=[pltpu.VMEM((RET_HEADS, qk_tot // RET_HEADS, v_tot // RET_HEADS), F32)] * 2,
        compiler_params=_cparams(("parallel", "arbitrary")),
    )(log_gamma, q, k, v, q, k, v)


def _readout_kernel(yf_ref, yb_ref, gate_ref, ng_ref, w_ref, x_ref, mod_ref, out_ref, *,
                    gate_first, group, gate_idx):
    y = yf_ref[0].astype(F32) + yb_ref[0].astype(F32)
    gate = _silu(gate_ref[0].astype(F32))
    if gate_first:
        y = y * gate
    pieces = []
    for j in range(y.shape[1] // group):
        yg = y[:, j * group:(j + 1) * group]
        ms = jnp.mean(yg * yg, axis=-1, keepdims=True)
        pieces.append(yg * lax.rsqrt(ms + NORM_EPS))
    y = jnp.concatenate(pieces, axis=1)
    y = y * ng_ref[...] if gate_first else y * gate
    o = _dot(y.astype(BF16), w_ref[...])
    out_ref[0] = x_ref[0] + mod_ref[0][gate_idx:gate_idx + 1] * o


def _readout(y_f, y_b, gate, norm_g, w_out, x, mod, ctx_tiles, gate_first, group):
    n_batch, s_len, inner = y_f.shape
    d = x.shape[2]
    n_tiles = s_len // ROW_TILE
    row = lambda b, t: (b, t, 0)
    return pl.pallas_call(
        functools.partial(_readout_kernel, gate_first=gate_first, group=group, gate_idx=2),
        out_shape=jax.ShapeDtypeStruct(x.shape, F32),
        grid=(n_batch, n_tiles),
        in_specs=[
            pl.BlockSpec((1, ROW_TILE, inner), row),
            pl.BlockSpec((1, ROW_TILE, inner), row),
            pl.BlockSpec((1, ROW_TILE, inner), row),
            pl.BlockSpec((1, inner), lambda b, t: (0, 0)),
            pl.BlockSpec((inner, d), lambda b, t: (0, 0)),
            pl.BlockSpec((1, ROW_TILE, d), row),
            pl.BlockSpec((1, 6, d), lambda b, t: (jnp.where(t < ctx_tiles, n_batch, b), 0, 0)),
        ],
        out_specs=pl.BlockSpec((1, ROW_TILE, d), row),
        compiler_params=_cparams(("parallel", "arbitrary")),
    )(y_f, y_b, gate, norm_g.reshape(1, inner), w_out, x, mod)


def _router_kernel(x_ref, g_ref, mod_ref, wr_ref, br_ref, h_ref, idx_ref, gate_ref, rank_ref,
                   count_ref, base_ref):
    @pl.when(jnp.logical_and(pl.program_id(0) == 0, pl.program_id(1) == 0))
    def _():
        base_ref[...] = jnp.zeros_like(base_ref)

    h = _norm_modulate(x_ref[0], g_ref[...], mod_ref[0], 3, 4)
    _store_token_tiles(h_ref, (), h)
    h_hi, h_lo = _split_bf16(h)
    w_hi, w_lo = _split_bf16(wr_ref[...])
    logits = _dot(h_hi, w_hi) + _dot(h_lo, w_hi) + _dot(h_hi, w_lo) + br_ref[...]
    lane = lax.broadcasted_iota(jnp.int32, logits.shape, 1).astype(F32)
    work = logits
    idx_out = jnp.zeros(logits.shape, F32)
    val_out = jnp.full(logits.shape, NEG_BIG, F32)
    picks = []
    for k in range(TOP_K):
        m = jnp.max(work, axis=-1, keepdims=True)
        first = jnp.min(jnp.where(work == m, lane, float(LANES)), axis=-1, keepdims=True)
        pick = lane == first
        picks.append(pick)
        idx_out = jnp.where(lane == k, first, idx_out)
        val_out = jnp.where(lane == k, m, val_out)
        work = jnp.where(pick, -jnp.inf, work)
    top = jnp.max(val_out, axis=-1, keepdims=True)
    e = jnp.exp(val_out - top)
    idx_ref[0] = idx_out.astype(jnp.int32)
    gate_ref[0] = e / jnp.sum(e, axis=-1, keepdims=True)

    rows = logits.shape[0]
    onehot = jnp.zeros(logits.shape, F32)
    for pick in picks:
        onehot = onehot + jnp.where(pick, 1.0, 0.0)
    r = lax.broadcasted_iota(jnp.int32, (rows, rows), 0)
    c = lax.broadcasted_iota(jnp.int32, (rows, rows), 1)
    strict = jnp.where(c < r, 1.0, 0.0).astype(BF16)
    earlier = _dot(strict, onehot.astype(BF16)) + base_ref[...]
    rank_out = jnp.zeros(logits.shape, F32)
    for k, pick in enumerate(picks):
        rk = jnp.sum(jnp.where(pick, earlier, 0.0), axis=-1, keepdims=True)
        rank_out = jnp.where(lane == k, rk, rank_out)
    rank_ref[0] = rank_out.astype(jnp.int32)
    base_ref[...] = base_ref[...] + jnp.sum(onehot, axis=0, keepdims=True)
    count_ref[...] = base_ref[...].astype(jnp.int32)


def _router(x, g, mod, w_router, b_router, ctx_tiles):
    n_batch, s_len, d = x.shape
    n_exp = w_router.shape[1]
    wr = jnp.zeros((d, LANES), F32).at[:, :n_exp].set(w_router)
    br = jnp.full((1, LANES), NEG_BIG, F32).at[0, :n_exp].set(b_router)
    row = lambda b, t: (b, t, 0)
    n_tiles = s_len // ROW_TILE
    return pl.pallas_call(
        _router_kernel,
        out_shape=[jax.ShapeDtypeStruct((n_batch * s_len * TOKEN_SUBLANES, LANES), F32),
                   jax.ShapeDtypeStruct((n_batch, s_len, LANES), jnp.int32),
                   jax.ShapeDtypeStruct((n_batch, s_len, LANES), F32),
                   jax.ShapeDtypeStruct((n_batch, s_len, LANES), jnp.int32),
                   jax.ShapeDtypeStruct((1, LANES), jnp.int32)],
        grid=(n_batch, s_len // ROW_TILE),
        in_specs=[
            pl.BlockSpec((1, ROW_TILE, d), row),
            pl.BlockSpec((1, d), lambda b, t: (0, 0)),
            pl.BlockSpec((1, 6, d), lambda b, t: (jnp.where(t < ctx_tiles, n_batch, b), 0, 0)),
            pl.BlockSpec((d, LANES), lambda b, t: (0, 0)),
            pl.BlockSpec((1, LANES), lambda b, t: (0, 0)),
        ],
        out_specs=[pl.BlockSpec((ROW_TILE * TOKEN_SUBLANES, LANES), lambda b, t: (b * n_tiles + t, 0)),
                   pl.BlockSpec((1, ROW_TILE, LANES), row),
                   pl.BlockSpec((1, ROW_TILE, LANES), row),
                   pl.BlockSpec((1, ROW_TILE, LANES), row),
                   pl.BlockSpec((1, LANES), lambda b, t: (0, 0))],
        scratch_shapes=[pltpu.VMEM((1, LANES), F32)],
        compiler_params=_cparams(("arbitrary", "arbitrary")),
    )(x, g.reshape(1, d), mod, wr, br)


def _on_slot(parity_of, fn):
    for s in range(2):
        @pl.when(parity_of % 2 == s)
        def _(s=s):
            fn(s)


def _indexed_row_pipeline(i, n, idx_hbm, idx_smem, isem, issue_rows):
    def idx_copy(tile, slot):
        return pltpu.make_async_copy(idx_hbm.at[tile], idx_smem[slot], isem.at[slot])

    @pl.when(i == 0)
    def _():
        idx_copy(0, 0).start()
        idx_copy(0, 0).wait()
        issue_rows(0, 0)

        @pl.when(n > 1)
        def _():
            idx_copy(1, 1).start()

    @pl.when(i + 1 < n)
    def _():
        def nxt(s):
            idx_copy(i + 1, s).wait()
            issue_rows(i + 1, s)
        _on_slot(i + 1, nxt)

    @pl.when(i + 2 < n)
    def _():
        _on_slot(i + 2, lambda s: idx_copy(i + 2, s).start())


def _dispatch_kernel(zblk_ref, nused_ref, pos_hbm, h_ref, xs_hbm, zero_buf, hbuf, idx_a, idx_b, zsem,
                     gsem, isem):
    i = pl.program_id(0)
    n = pl.num_programs(0)
    n_rows = TOP_K * ROW_TILE
    block_rows = EXPERT_ROWS * TOKEN_SUBLANES

    @pl.when(i == 0)
    def _():
        zero_buf[...] = jnp.zeros_like(zero_buf)

        def zero_block(blk):
            first = pl.multiple_of(blk * block_rows, block_rows)
            cp = pltpu.make_async_copy(zero_buf, xs_hbm.at[pl.ds(first, block_rows), :], zsem)
            cp.start()
            cp.wait()

        for e in range(zblk_ref.shape[0]):
            @pl.when(zblk_ref[e] >= 0)
            def _(e=e):
                zero_block(zblk_ref[e])

        def tail(blk, carry):
            zero_block(blk)
            return carry
        lax.fori_loop(nused_ref[0], xs_hbm.shape[0] // block_rows, tail, 0)

    def idx_copy(tile, slot):
        return pltpu.make_async_copy(pos_hbm.at[tile], (idx_a, idx_b)[slot], isem.at[slot])

    def wait_rows(slot):
        span = pl.ds(0, n_rows * TOKEN_SUBLANES)
        pltpu.make_async_copy(xs_hbm.at[span, :], xs_hbm.at[span, :], gsem.at[slot]).wait()

    @pl.when(i == 0)
    def _():
        idx_copy(0, 0).start()

    def step(slot):
        idx = (idx_a, idx_b)[slot]
        idx_copy(i, slot).wait()

        @pl.when(i >= 2)
        def _():
            wait_rows(slot)

        hbuf[slot] = h_ref[...]

        def body(it, carry):
            for j in range(DMA_UNROLL):
                r = it * DMA_UNROLL + j
                tok = it * (DMA_UNROLL // TOP_K) + j // TOP_K
                src = pl.multiple_of(tok * TOKEN_SUBLANES, TOKEN_SUBLANES)
                dst = pl.multiple_of(idx[r], TOKEN_SUBLANES)
                pltpu.make_async_copy(hbuf.at[slot, pl.ds(src, TOKEN_SUBLANES), :],
                                      xs_hbm.at[pl.ds(dst, TOKEN_SUBLANES), :],
                                      gsem.at[slot]).start(priority=j % 2)
            return carry
        lax.fori_loop(0, n_rows // DMA_UNROLL, body, 0)
    _on_slot(i, step)

    @pl.when(i + 1 < n)
    def _():
        _on_slot(i + 1, lambda s: idx_copy(i + 1, s).start())

    @pl.when(i == n - 1)
    def _():
        @pl.when(n >= 2)
        def _():
            _on_slot(i + 1, wait_rows)
        _on_slot(i, wait_rows)


def _dispatch(pos_tiles, zero_blocks, n_used, h_tiles, cap):
    n_tiles = pos_tiles.shape[0]
    tile_rows = ROW_TILE * TOKEN_SUBLANES
    grid_spec = pltpu.PrefetchScalarGridSpec(
        num_scalar_prefetch=2,
        grid=(n_tiles,),
        in_specs=[pl.BlockSpec(memory_space=pl.ANY),
                  pl.BlockSpec((tile_rows, LANES), lambda i, zb, nu: (i, 0))],
        out_specs=pl.BlockSpec(memory_space=pl.ANY),
        scratch_shapes=[
            pltpu.VMEM((EXPERT_ROWS * TOKEN_SUBLANES, LANES), F32),
            pltpu.VMEM((2, tile_rows, LANES), F32),
            pltpu.SMEM((TOP_K * ROW_TILE,), jnp.int32),
            pltpu.SMEM((TOP_K * ROW_TILE,), jnp.int32),
            pltpu.SemaphoreType.DMA,
            pltpu.SemaphoreType.DMA((2,)),
            pltpu.SemaphoreType.DMA((2,)),
        ],
    )
    return pl.pallas_call(
        _dispatch_kernel,
        out_shape=jax.ShapeDtypeStruct((cap * TOKEN_SUBLANES, LANES), F32),
        grid_spec=grid_spec,
        compiler_params=_cparams(("arbitrary",)),
    )(zero_blocks, n_used, pos_tiles, h_tiles)


def _expert_kernel(bexp_ref, nused_ref, x_ref, w1_ref, b1_ref, w2_ref, b2_ref, out_ref, w1p, w2p):
    i = pl.program_id(0)
    prev = bexp_ref[jnp.maximum(i - 1, 0)]
    f2 = w1_ref.shape[3]

    @pl.when(jnp.logical_or(i == 0, bexp_ref[i] != prev))
    def _():
        src = lax.broadcasted_iota(jnp.int32, (MXU_DIM, MXU_DIM), 0)
        dst = lax.broadcasted_iota(jnp.int32, (MXU_DIM, MXU_DIM), 1)
        want = jnp.where(dst < LANES, 2 * dst, 2 * (dst - LANES) + 1)
        perm = jnp.where(src == want, 1.0, 0.0).astype(BF16)
        for c in range(f2 // MXU_DIM):
            blk = w1_ref[0, 0, :, c * MXU_DIM:(c + 1) * MXU_DIM].astype(BF16)
            w1p[:, c * MXU_DIM:(c + 1) * MXU_DIM] = _dot(blk, perm).astype(BF16)
        w2p[...] = w2_ref[0, 0].astype(BF16)

    @pl.when(i < nused_ref[0])
    def _():
        x = jnp.concatenate(_load_token_tiles(x_ref, (), 0, EXPERT_ROWS), axis=1).astype(BF16)
        hdn = _dot(x, w1p[...]) + b1_ref[0]
        acts = []
        for c in range(f2 // MXU_DIM):
            glu = jnp.minimum(hdn[:, c * MXU_DIM:c * MXU_DIM + LANES], SWIGLU_LIMIT)
            lin = jnp.clip(hdn[:, c * MXU_DIM + LANES:(c + 1) * MXU_DIM], -SWIGLU_LIMIT, SWIGLU_LIMIT)
            acts.append(glu * _sigmoid(SWIGLU_ALPHA * glu) * (lin + 1.0))
        act = jnp.concatenate(acts, axis=1).astype(BF16)
        _store_token_tiles(out_ref, (), _dot(act, w2p[...]) + b2_ref[0])

    @pl.when(i >= nused_ref[0])
    def _():
        out_ref[...] = jnp.zeros_like(out_ref)


def _experts(x_sorted, block_expert, n_used, layer, w1_all, b1p, w2_all, b2):
    n_blocks = block_expert.shape[0]
    _, n_exp, d, f2 = w1_all.shape
    f = w2_all.shape[2]
    block_rows = EXPERT_ROWS * TOKEN_SUBLANES
    wspec = lambda shape: pl.BlockSpec(shape, lambda i, be, nu: (be[i], 0, 0))
    wspec_all = lambda shape: pl.BlockSpec(shape, lambda i, be, nu: (layer, be[i], 0, 0))
    grid_spec = pltpu.PrefetchScalarGridSpec(
        num_scalar_prefetch=2,
        grid=(n_blocks,),
        in_specs=[
            pl.BlockSpec((block_rows, LANES), lambda i, be, nu: (jnp.minimum(i, nu[0] - 1), 0)),
            wspec_all((1, 1, d, f2)), wspec((1, 1, f2)), wspec_all((1, 1, f, d)), wspec((1, 1, d)),
        ],
        out_specs=pl.BlockSpec((block_rows, LANES), lambda i, be, nu: (i, 0)),
        scratch_shapes=[pltpu.VMEM((d, f2), BF16), pltpu.VMEM((f, d), BF16)],
    )
    return pl.pallas_call(
        _expert_kernel,
        out_shape=jax.ShapeDtypeStruct((n_blocks * block_rows, LANES), F32),
        grid_spec=grid_spec,
        compiler_params=_cparams(("arbitrary",)),
    )(block_expert, n_used, x_sorted, w1_all, b1p, w2_all, b2.reshape(n_exp, 1, d))


def _combine_kernel(pos_hbm, y_hbm, gate_ref, x_ref, mod_ref, out_ref, ybuf, idx_a, idx_b, gsem, isem):
    i = pl.program_id(0)
    n = pl.num_programs(0)
    n_rows = TOP_K * ROW_TILE

    def issue_rows(tile, slot):
        idx = (idx_a, idx_b)[slot]

        def body(it, carry):
            for j in range(DMA_UNROLL):
                r = it * DMA_UNROLL + j
                src = pl.multiple_of(idx[r], TOKEN_SUBLANES)
                dst = pl.multiple_of(r * TOKEN_SUBLANES, TOKEN_SUBLANES)
                pltpu.make_async_copy(y_hbm.at[pl.ds(src, TOKEN_SUBLANES), :],
                                      ybuf.at[slot, pl.ds(dst, TOKEN_SUBLANES), :],
                                      gsem.at[slot]).start(priority=j % 2)
            return carry
        lax.fori_loop(0, n_rows // DMA_UNROLL, body, 0)

    def wait_rows(slot):
        pltpu.make_async_copy(y_hbm.at[pl.ds(0, n_rows * TOKEN_SUBLANES), :], ybuf.at[slot],
                              gsem.at[slot]).wait()

    _indexed_row_pipeline(i, n, pos_hbm, (idx_a, idx_b), isem, issue_rows)
    _on_slot(i, wait_rows)

    def compute(slot):
        g2 = mod_ref[0][5:6]
        sub = COMBINE_ROWS
        for rb in range(ROW_TILE // sub):
            rows = slice(rb * sub, (rb + 1) * sub)
            gate = gate_ref[rows, :]
            gk = [jnp.broadcast_to(gate[:, k:k + 1], (sub, LANES)) for k in range(TOP_K)]
            for s in range(TOKEN_SUBLANES):
                acc = None
                for k in range(TOP_K):
                    first = (k * ROW_TILE + rb * sub) * TOKEN_SUBLANES + s
                    part = ybuf[slot, pl.ds(first, sub, stride=TOKEN_SUBLANES), :] * gk[k]
                    acc = part if acc is None else acc + part
                cols = slice(s * LANES, (s + 1) * LANES)
                out_ref[rows, cols] = x_ref[rows, cols] + g2[:, cols] * acc
    _on_slot(i, compute)


def _combine(pos_tiles, y_rows, gate_flat, x_flat, mod, n_batch, tiles_per_batch, ctx_tiles):
    n_tiles = pos_tiles.shape[0]
    d = x_flat.shape[1]

    def mod_map(i):
        b = i // tiles_per_batch
        return (jnp.where(i % tiles_per_batch < ctx_tiles, n_batch, b), 0, 0)

    return pl.pallas_call(
        _combine_kernel,
        out_shape=jax.ShapeDtypeStruct(x_flat.shape, F32),
        grid=(n_tiles,),
        in_specs=[
            pl.BlockSpec(memory_space=pl.ANY),
            pl.BlockSpec(memory_space=pl.ANY),
            pl.BlockSpec((ROW_TILE, LANES), lambda i: (i, 0)),
            pl.BlockSpec((ROW_TILE, d), lambda i: (i, 0)),
            pl.BlockSpec((1, 6, d), mod_map),
        ],
        out_specs=pl.BlockSpec((ROW_TILE, d), lambda i: (i, 0)),
        scratch_shapes=[
            pltpu.VMEM((2, TOP_K * ROW_TILE * TOKEN_SUBLANES, LANES), F32),
            pltpu.SMEM((TOP_K * ROW_TILE,), jnp.int32),
            pltpu.SMEM((TOP_K * ROW_TILE,), jnp.int32),
            pltpu.SemaphoreType.DMA((2,)),
            pltpu.SemaphoreType.DMA((2,)),
        ],
        compiler_params=_cparams(("arbitrary",)),
    )(pos_tiles, y_rows, gate_flat, x_flat, mod)


def _moe(x, g, mod, w_router, b_router, layer, w1_all, b1p, w2_all, b2, ctx_tiles):
    n_batch, s_len, d = x.shape
    n_tok = n_batch * s_len
    n_exp = w1_all.shape[1]
    n_tiles = n_tok // ROW_TILE
    h, idx, gate, rank, count = _router(x, g, mod, w_router, b_router, ctx_tiles)

    counts = count[0, :n_exp]
    padded = (counts + EXPERT_ROWS - 1) // EXPERT_ROWS * EXPERT_ROWS
    pend = jnp.cumsum(padded)
    pstart = pend - padded
    n_blocks = -(-(n_tok * TOP_K) // EXPERT_ROWS) + n_exp
    block_start = jnp.arange(n_blocks, dtype=jnp.int32) * EXPERT_ROWS
    block_expert = jnp.minimum(jnp.sum(block_start[:, None] >= pend[None, :], axis=1),
                               n_exp - 1).astype(jnp.int32)
    n_used = (pend[-1] // EXPERT_ROWS).astype(jnp.int32).reshape(1)
    zero_blocks = jnp.where(padded > counts, pend // EXPERT_ROWS - 1, -1).astype(jnp.int32)

    top_i = idx[..., :TOP_K].reshape(n_tok, TOP_K)
    pos = ((pstart[top_i] + rank[..., :TOP_K].reshape(n_tok, TOP_K)) * TOKEN_SUBLANES).astype(jnp.int32)
    pos_token_major = pos.reshape(n_tiles, ROW_TILE * TOP_K)
    pos_slot_major = pos.reshape(n_tiles, ROW_TILE, TOP_K).transpose(0, 2, 1).reshape(n_tiles, -1)

    x_sorted = _dispatch(pos_token_major, zero_blocks, n_used, h, n_blocks * EXPERT_ROWS)
    y_rows = _experts(x_sorted, block_expert, n_used, layer, w1_all, b1p, w2_all, b2)
    out = _combine(pos_slot_major, y_rows, gate.reshape(n_tok, LANES), x.reshape(n_tok, d), mod,
                   n_batch, s_len // ROW_TILE, ctx_tiles)
    return out.reshape(n_batch, s_len, d)


def _final_norm_kernel(x_ref, g_ref, out_ref):
    x = x_ref[0]
    ms = jnp.mean(x * x, axis=-1, keepdims=True)
    out_ref[0] = x * lax.rsqrt(ms + NORM_EPS) * g_ref[...]


def _final_norm(x, g, ctx_tiles, seq_len):
    n_batch, _, d = x.shape
    return pl.pallas_call(
        _final_norm_kernel,
        out_shape=jax.ShapeDtypeStruct((n_batch, seq_len, d), F32),
        grid=(n_batch, seq_len // ROW_TILE),
        in_specs=[pl.BlockSpec((1, ROW_TILE, d), lambda b, t: (b, t + ctx_tiles, 0)),
                  pl.BlockSpec((1, d), lambda b, t: (0, 0))],
        out_specs=pl.BlockSpec((1, ROW_TILE, d), lambda b, t: (b, t, 0)),
        compiler_params=_cparams(("parallel", "arbitrary")),
    )(x, g.reshape(1, d))


def _rope_tables(seq_len, ctx_len, half_dim):
    rows = seq_len // GRID_W
    t = jnp.arange(seq_len)
    row_pos = (t // GRID_W).astype(F32) - (rows - 1) / 2.0
    col_pos = (t % GRID_W).astype(F32) - (GRID_W - 1) / 2.0
    inv_freq = ROPE_BASE ** (-jnp.arange(0, half_dim, 2, dtype=F32) / half_dim)

    def tables(pos):
        ang = pos[:, None] * inv_freq[None, :]
        c, s = jnp.cos(ang), jnp.sin(ang)
        return jnp.concatenate([c, c], axis=1), jnp.concatenate([-s, s], axis=1)

    cr, sr = tables(row_pos)
    cc, sc = tables(col_pos)
    cos = jnp.concatenate([cr, cc], axis=1)
    sin = jnp.concatenate([sr, sc], axis=1)
    cos = jnp.concatenate([jnp.ones((ctx_len, cos.shape[1]), F32), cos], axis=0)
    sin = jnp.concatenate([jnp.zeros((ctx_len, sin.shape[1]), F32), sin], axis=0)
    return cos, sin


def kernel(x, c, ctx, c_ctx, w_mod, b_mod, norm1_g, norm2_g, ssd_w_in, ssd_conv_w, ssd_conv_b,
           ssd_dt_bias_f, ssd_dt_bias_b, ssd_a_log_f, ssd_a_log_b, ssd_d, ssd_norm_g, ssd_w_out,
           ret_w_in, ret_scale_f, ret_scale_b, ret_w_out, router_w, router_b,
           moe_w1, moe_b1, moe_w2, moe_b2, final_g):
    n_batch, seq_len, d = x.shape
    ctx_len = ctx.shape[1]
    depth = w_mod.shape[0]
    assert ctx_len % ROW_TILE == 0 and seq_len % ROW_TILE == 0
    ctx_tiles = ctx_len // ROW_TILE
    ctx_chunks = ctx_len // CHUNK

    n_heads = ssd_d.shape[1]
    ssd_inner = n_heads * SSD_HEAD_DIM
    bc_w = SSD_GROUPS * SSD_STATE
    conv_dim = ssd_inner + 2 * bc_w
    assert n_heads == LANES // 4 and ssd_inner % bc_w == 0

    ret_qk = (ret_w_in.shape[2] - 2 * ret_w_out.shape[1]) // 2
    ret_v = ret_w_out.shape[1]
    qk_dim = ret_qk // RET_HEADS
    assert qk_dim == 2 * LANES

    n_exp, _, f2 = moe_w1.shape[1:]
    assert f2 % MXU_DIM == 0 and d == TOKEN_SUBLANES * LANES

    cond = jnp.concatenate([c, c_ctx[None, :], jnp.zeros((16 - n_batch - 1, d), F32)], axis=0)
    mod_all = _modulation(cond, w_mod, b_mod).reshape(depth, 16, 6, d)

    xs = jnp.concatenate([ctx, x], axis=1)
    cos, sin = _rope_tables(seq_len, ctx_len, qk_dim // 2)

    for i in range(depth):
        j = i // 2
        mod = mod_all[i]
        if i % 2 == 0:
            w_in = ssd_w_in[j]
            w_dt = w_in[:, ssd_inner + conv_dim:].astype(BF16)
            weights = [w_in[:, :ssd_inner].astype(BF16),
                       w_in[:, ssd_inner:ssd_inner + conv_dim].astype(BF16),
                       jnp.concatenate([w_dt, w_dt], axis=1)]
            z, xbc, dt_raw = _in_proj(xs, norm1_g[i], mod, weights, [BF16, BF16, F32], ctx_tiles)
            dt_bias2 = jnp.concatenate([ssd_dt_bias_f[j], ssd_dt_bias_b[j]])
            dt_bias4 = jnp.concatenate([dt_bias2, dt_bias2]).reshape(1, LANES)
            a_log4 = jnp.concatenate([jnp.zeros((2 * n_heads,), F32), ssd_a_log_f[j],
                                      ssd_a_log_b[j]]).reshape(1, LANES)
            xbc_c, dtla = _ssd_conv(xbc, dt_raw, ssd_conv_w[j], ssd_conv_b[j], dt_bias4, a_log4,
                                    ctx_chunks)
            skip = jnp.repeat(ssd_d[j], SSD_HEAD_DIM).reshape(1, ssd_inner)
            y_f, y_b = _ssd_scan(xbc_c, dtla, skip, ssd_inner, ctx_chunks)
            xs = _readout(y_f, y_b, z, ssd_norm_g[j], ssd_w_out[j].astype(BF16), xs, mod,
                          ctx_tiles, True, ssd_inner // SSD_GROUPS)
        else:
            w_in = ret_w_in[j]
            bounds = [0, ret_qk, 2 * ret_qk, 2 * ret_qk + ret_v, 2 * ret_qk + 2 * ret_v]
            weights = [w_in[:, lo:hi].astype(BF16) for lo, hi in zip(bounds[:-1], bounds[1:])]
            q, k, v, gt = _in_proj(xs, norm1_g[i], mod, weights, [BF16] * 4, ctx_tiles,
                                   rope=(cos, sin, (0, 1)), k_scale=qk_dim ** -0.5)
            log_gamma = jnp.stack([jnp.log1p(-jnp.exp2(-ret_scale_f[j].astype(F32))),
                                   jnp.log1p(-jnp.exp2(-ret_scale_b[j].astype(F32)))])
            y_f, y_b = _ret_scan(q, k, v, log_gamma, ctx_chunks)
            xs = _readout(y_f, y_b, gt, jnp.ones((ret_v,), F32), ret_w_out[j].astype(BF16), xs, mod,
                          ctx_tiles, False, ret_v // RET_HEADS)
        b1p = moe_b1[i].reshape(n_exp, f2 // MXU_DIM, LANES, 2).transpose(0, 1, 3, 2).reshape(n_exp, 1, f2)
        xs = _moe(xs, norm2_g[i], mod, router_w[i], router_b[i], i, moe_w1, b1p, moe_w2, moe_b2[i],
                  ctx_tiles)
    return _final_norm(xs, final_g, ctx_tiles, seq_len)
```

```python
import functools

import jax
import jax.numpy as jnp
from jax import lax
from jax.experimental import pallas as pl
from jax.experimental.pallas import tpu as pltpu

F32 = jnp.float32
BF16 = jnp.bfloat16

NORM_EPS = 1e-6
CHUNK = 128
RET_CHUNK = 256
GRID_W = 64
ROPE_BASE = 10000.0

SSD_HEAD_DIM = 64
SSD_GROUPS = 4
SSD_STATE = 128
SSD_CONV = 5

RET_HEADS = 4

TOP_K = 4
SWIGLU_ALPHA = 1.702
SWIGLU_LIMIT = 7.0

LANES = 128
MXU_DIM = 256
ROW_TILE = 256
EXPERT_ROWS = 512
HALO = 16
TOKEN_SUBLANES = 8
DMA_UNROLL = 8
COMBINE_ROWS = 64
VMEM_LIMIT = 56 * 1024 * 1024
NEG_BIG = -1e30


def _cparams(sem):
    return pltpu.CompilerParams(dimension_semantics=sem, vmem_limit_bytes=VMEM_LIMIT)


def _split_bf16(a):
    hi = a.astype(BF16)
    lo = (a - hi.astype(F32)).astype(BF16)
    return hi, lo


def _dot(a, b):
    return jnp.dot(a, b, preferred_element_type=F32)


def _dot_tn(a, b):
    return lax.dot_general(a, b, (((0,), (0,)), ((), ())), preferred_element_type=F32)


def _dot_nt(a, b):
    return lax.dot_general(a, b, (((1,), (1,)), ((), ())), preferred_element_type=F32)


def _dot_split_lhs(a, b_bf16):
    hi, lo = _split_bf16(a)
    return _dot(hi, b_bf16) + _dot(lo, b_bf16)


def _store_token_tiles(ref, lead, value):
    rows = value.shape[0]
    for s in range(TOKEN_SUBLANES):
        ref[lead + (pl.ds(s, rows, stride=TOKEN_SUBLANES), slice(None))] = value[:, s * LANES:(s + 1) * LANES]


def _load_token_tiles(ref, lead, first_row, rows):
    return [ref[lead + (pl.ds(first_row + s, rows, stride=TOKEN_SUBLANES), slice(None))]
            for s in range(TOKEN_SUBLANES)]


def _sigmoid(x):
    return 1.0 / (1.0 + jnp.exp(-x))


def _silu(x):
    return x * _sigmoid(x)


def _mod_kernel(cond_ref, w_ref, b_ref, out_ref):
    s = _silu(cond_ref[...])
    s_hi, s_lo = _split_bf16(s)
    w_hi, w_lo = _split_bf16(w_ref[0])
    acc = _dot(s_hi, w_hi) + _dot(s_lo, w_hi) + _dot(s_hi, w_lo)
    out_ref[0] = acc + b_ref[0]


def _modulation(cond, w_mod, b_mod):
    depth, d, n = w_mod.shape
    rows = cond.shape[0]
    tn = 1024
    return pl.pallas_call(
        _mod_kernel,
        out_shape=jax.ShapeDtypeStruct((depth, rows, n), F32),
        grid=(depth, n // tn),
        in_specs=[
            pl.BlockSpec((rows, d), lambda l, j: (0, 0)),
            pl.BlockSpec((1, d, tn), lambda l, j: (l, 0, j)),
            pl.BlockSpec((1, 1, tn), lambda l, j: (l, 0, j)),
        ],
        out_specs=pl.BlockSpec((1, rows, tn), lambda l, j: (l, 0, j)),
        compiler_params=_cparams(("arbitrary", "arbitrary")),
    )(cond, w_mod, b_mod.reshape(depth, 1, n))


def _norm_modulate(x, g, mod, shift_idx, scale_idx):
    ms = jnp.mean(x * x, axis=-1, keepdims=True)
    h = x * lax.rsqrt(ms + NORM_EPS) * g
    return h * (1.0 + mod[scale_idx:scale_idx + 1]) + mod[shift_idx:shift_idx + 1]


def _in_proj_kernel(*refs, n_out, rope_outs, k_scale):
    x_ref, g_ref, mod_ref = refs[:3]
    w_refs = refs[3:3 + n_out]
    if rope_outs:
        cos_ref, sin_ref = refs[3 + n_out:5 + n_out]
        out_refs = refs[5 + n_out:]
    else:
        out_refs = refs[3 + n_out:]
    h = _norm_modulate(x_ref[0], g_ref[...], mod_ref[0], 0, 1).astype(BF16)
    for idx, w_ref in enumerate(w_refs):
        o = _dot(h, w_ref[...])
        if idx in rope_outs:
            cos = cos_ref[...]
            sin = sin_ref[...]
            pieces = []
            for j in range(o.shape[1] // LANES):
                u = o[:, j * LANES:(j + 1) * LANES]
                half = (j % 2) * LANES
                r = (u * cos[:, half:half + LANES]
                     + pltpu.roll(u, LANES // 2, axis=1) * sin[:, half:half + LANES])
                pieces.append(r)
            o = jnp.concatenate(pieces, axis=1)
            if idx == rope_outs[-1]:
                o = o * k_scale
        out_refs[idx][0] = o.astype(out_refs[idx].dtype)


def _in_proj(x, g, mod, weights, out_dtypes, ctx_tiles, rope=None, k_scale=1.0):
    n_batch, s_len, d = x.shape
    n_tiles = s_len // ROW_TILE
    in_specs = [
        pl.BlockSpec((1, ROW_TILE, d), lambda b, t: (b, t, 0)),
        pl.BlockSpec((1, d), lambda b, t: (0, 0)),
        pl.BlockSpec((1, 6, d), lambda b, t: (jnp.where(t < ctx_tiles, n_batch, b), 0, 0)),
    ] + [pl.BlockSpec(w.shape, lambda b, t: (0, 0)) for w in weights]
    args = [x, g.reshape(1, d), mod] + list(weights)
    splits = [(None, w.shape[1]) for w in weights]
    rope_outs = ()
    if rope is not None:
        cos, sin, rope_outs = rope
        in_specs += [pl.BlockSpec((ROW_TILE, cos.shape[1]), lambda b, t: (t, 0)),
                     pl.BlockSpec((ROW_TILE, sin.shape[1]), lambda b, t: (t, 0))]
        args += [cos, sin]
    out_shape = [jax.ShapeDtypeStruct((n_batch, s_len, wd), dt)
                 for (_, wd), dt in zip(splits, out_dtypes)]
    out_specs = [pl.BlockSpec((1, ROW_TILE, wd), lambda b, t: (b, t, 0)) for _, wd in splits]
    return pl.pallas_call(
        functools.partial(_in_proj_kernel, n_out=len(weights), rope_outs=tuple(rope_outs),
                          k_scale=k_scale),
        out_shape=out_shape,
        grid=(n_batch, n_tiles),
        in_specs=in_specs,
        out_specs=out_specs,
        compiler_params=_cparams(("parallel", "arbitrary")),
    )(*args)


def _ssd_conv_kernel(cur_ref, prev_ref, next_ref, dt_ref, cw_ref, cb_ref, dtb_ref, alog_ref,
                     xbc_ref, dtla_ref, ext_ref, *, ctx_chunks, n_chunks):
    c = pl.program_id(1)
    has_prev = jnp.logical_and(c != 0, c != ctx_chunks)
    has_next = jnp.logical_and(c != ctx_chunks - 1, c != n_chunks - 1)
    pad = SSD_CONV // 2
    ext_ref[0:8, :] = prev_ref[0].astype(F32)[HALO - 8:HALO, :] * jnp.where(has_prev, 1.0, 0.0)
    ext_ref[8:8 + CHUNK, :] = cur_ref[0].astype(F32)
    ext_ref[8 + CHUNK:16 + CHUNK, :] = next_ref[0].astype(F32)[0:8, :] * jnp.where(has_next, 1.0, 0.0)
    cw = cw_ref[...]
    ext_rows = CHUNK + 16
    taps = [k for k in range(SSD_CONV) if k != pad]
    sel_r = lax.broadcasted_iota(jnp.int32, (len(taps) * CHUNK, ext_rows), 0)
    sel_c = lax.broadcasted_iota(jnp.int32, (len(taps) * CHUNK, ext_rows), 1)
    want = jnp.zeros_like(sel_r)
    for n, k in enumerate(taps):
        in_tap = jnp.logical_and(sel_r >= n * CHUNK, sel_r < (n + 1) * CHUNK)
        want = jnp.where(in_tap, sel_r - n * CHUNK + 8 - pad + k, want)
    select = jnp.where(sel_c == want, 1.0, 0.0).astype(BF16)
    shifted = _dot(select, ext_ref[...].astype(BF16))
    acc = ext_ref[8:8 + CHUNK, :] * cw[pad:pad + 1, :] + cb_ref[...]
    for n, k in enumerate(taps):
        acc = acc + shifted[n * CHUNK:(n + 1) * CHUNK] * cw[k:k + 1, :]
    xbc_ref[0] = _silu(acc).astype(xbc_ref.dtype)
    raw = dt_ref[0] + dtb_ref[...]
    dt = jnp.maximum(raw, 0.0) + jnp.log(1.0 + jnp.exp(-jnp.abs(raw)))
    lane = lax.broadcasted_iota(jnp.int32, dt.shape, 1)
    a = -jnp.exp(alog_ref[...])
    dtla_ref[0] = jnp.where(lane < LANES // 2, dt, dt * a)


def _ssd_conv(xbc, dt_raw, conv_w, conv_b, dt_bias4, a_log4, ctx_chunks):
    n_batch, s_len, cdim = xbc.shape
    n_chunks = s_len // CHUNK
    per = CHUNK // HALO
    last_halo = s_len // HALO - 1
    return pl.pallas_call(
        functools.partial(_ssd_conv_kernel, ctx_chunks=ctx_chunks, n_chunks=n_chunks),
        out_shape=[jax.ShapeDtypeStruct((n_batch, s_len, cdim), BF16),
                   jax.ShapeDtypeStruct((n_batch, s_len, LANES), F32)],
        grid=(n_batch, n_chunks),
        in_specs=[
            pl.BlockSpec((1, CHUNK, cdim), lambda b, c: (b, c, 0)),
            pl.BlockSpec((1, HALO, cdim), lambda b, c: (b, jnp.maximum(c * per - 1, 0), 0)),
            pl.BlockSpec((1, HALO, cdim), lambda b, c: (b, jnp.minimum(c * per + per, last_halo), 0)),
            pl.BlockSpec((1, CHUNK, LANES), lambda b, c: (b, c, 0)),
            pl.BlockSpec((SSD_CONV, cdim), lambda b, c: (0, 0)),
            pl.BlockSpec((1, cdim), lambda b, c: (0, 0)),
            pl.BlockSpec((1, LANES), lambda b, c: (0, 0)),
            pl.BlockSpec((1, LANES), lambda b, c: (0, 0)),
        ],
        out_specs=[pl.BlockSpec((1, CHUNK, cdim), lambda b, c: (b, c, 0)),
                   pl.BlockSpec((1, CHUNK, LANES), lambda b, c: (b, c, 0))],
        scratch_shapes=[pltpu.VMEM((CHUNK + 16, cdim), F32)],
        compiler_params=_cparams(("parallel", "arbitrary")),
    )(xbc, xbc, xbc, dt_raw, conv_w, conv_b.reshape(1, cdim), dt_bias4, a_log4)


def _bwd_chunk(s, ctx_chunks, n_chunks):
    return jnp.where(s < ctx_chunks, ctx_chunks - 1 - s, n_chunks - 1 - (s - ctx_chunks))


def _ssd_chunk(xs_ref, b_ref, c_ref, dtla_ref, skip, state_ref, y_ref, *, reverse):
    n_heads = xs_ref.shape[2] // SSD_HEAD_DIM
    hp = xs_ref.shape[2]
    gw = hp // SSD_GROUPS
    off_dt = LANES // 4 if reverse else 0
    off_la = off_dt + LANES // 2

    row = lax.broadcasted_iota(jnp.int32, (CHUNK, CHUNK), 0)
    col = lax.broadcasted_iota(jnp.int32, (CHUNK, CHUNK), 1)
    before = (col >= row) if reverse else (col <= row)
    tri = jnp.where(before, 1.0, 0.0).astype(BF16)
    tri_t = jnp.where((row >= col) if reverse else (row <= col), 1.0, 0.0).astype(BF16)

    dtla = dtla_ref[0]
    hi, lo = _split_bf16(dtla)
    cum = _dot(tri, hi) + _dot(tri, lo)
    cum_t = _dot_tn(hi, tri_t) + _dot_tn(lo, tri_t)
    cum = jnp.where(lax.broadcasted_iota(jnp.int32, cum.shape, 1) >= LANES // 2, cum, 0.0)
    last = 0 if reverse else CHUNK - 1
    total = cum[last:last + 1, :]

    sec = lax.broadcasted_iota(jnp.int32, (LANES, hp), 0)
    head = lax.broadcasted_iota(jnp.int32, (LANES, hp), 1) // SSD_HEAD_DIM
    e_dt = jnp.where(sec == head + off_dt, 1.0, 0.0).astype(BF16)
    e_la = jnp.where(sec == head + off_la, 1.0, 0.0).astype(BF16)

    xs = xs_ref[0].astype(F32)
    lane_q = lax.broadcasted_iota(jnp.int32, (CHUNK, LANES), 1)
    is_dt = jnp.logical_and(lane_q >= off_dt, lane_q < off_dt + n_heads)
    is_la = jnp.logical_and(lane_q >= off_la, lane_q < off_la + n_heads)
    stacked = jnp.concatenate([jnp.where(is_dt, dtla, 0.0),
                               jnp.where(is_la, jnp.exp(cum), 0.0),
                               jnp.where(is_la, jnp.exp(total - cum), 0.0)], axis=0).astype(BF16)
    spread = _dot(stacked, e_dt + e_la)
    xdt = xs * spread[0:CHUNK]
    xdt_b = xdt.astype(BF16)
    dec_in = spread[CHUNK:2 * CHUNK]
    dec_end = spread[2 * CHUNK:3 * CHUNK]
    dec_all = _dot_split_lhs(jnp.exp(total), e_la)
    xdec_b = (xdt * dec_end).astype(BF16)

    cm = c_ref[0]
    bm = b_ref[0]
    lane = lax.broadcasted_iota(jnp.int32, (CHUNK, LANES), 1)
    low = lane < SSD_HEAD_DIM
    heads_per_group = n_heads // SSD_GROUPS
    for g in range(SSD_GROUPS):
        cg = cm[:, g * SSD_STATE:(g + 1) * SSD_STATE]
        bg = bm[:, g * SSD_STATE:(g + 1) * SSD_STATE]
        scores = _dot_nt(cg, bg)
        cols = slice(g * gw, (g + 1) * gw)
        h_enter = state_ref[:, cols]
        y_inter = _dot(cg, h_enter.astype(BF16)) * dec_in[:, cols]
        state_ref[:, cols] = h_enter * dec_all[:, cols] + _dot_tn(bg, xdec_b[:, cols])
        for pair in range(heads_per_group // 2):
            base = g * gw + pair * LANES
            rhs = xdt_b[:, base:base + LANES]
            ms = []
            for k in range(2):
                h = g * heads_per_group + pair * 2 + k
                seg = cum[:, off_la + h:off_la + h + 1] - cum_t[off_la + h:off_la + h + 1, :]
                ms.append((scores * jnp.exp(jnp.where(before, seg, NEG_BIG))).astype(BF16))
            zero = jnp.zeros_like(rhs)
            rhs2 = jnp.concatenate([jnp.where(low, rhs, zero), jnp.where(low, zero, rhs)], axis=0)
            y = _dot(jnp.concatenate(ms, axis=1), rhs2) + y_inter[:, pair * LANES:(pair + 1) * LANES]
            if skip is not None:
                y = y + xs[:, base:base + LANES] * skip[:, base:base + LANES]
            y_ref[0, :, base:base + LANES] = y.astype(y_ref.dtype)


def _ssd_scan_kernel(xs_f, b_f, c_f, dtla_f, xs_b, b_b, c_b, dtla_b, skip_ref,
                     yf_ref, yb_ref, st_f, st_b):
    @pl.when(pl.program_id(1) == 0)
    def _():
        st_f[...] = jnp.zeros_like(st_f)
        st_b[...] = jnp.zeros_like(st_b)

    _ssd_chunk(xs_f, b_f, c_f, dtla_f, skip_ref[...], st_f, yf_ref, reverse=False)
    _ssd_chunk(xs_b, b_b, c_b, dtla_b, None, st_b, yb_ref, reverse=True)


def _ssd_scan(xbc, dtla, skip, hp, ctx_chunks):
    n_batch, s_len, _ = xbc.shape
    n_chunks = s_len // CHUNK
    bc_w = SSD_GROUPS * SSD_STATE
    b_blk = hp // bc_w
    fwd = lambda s: s
    bwd = lambda s: _bwd_chunk(s, ctx_chunks, n_chunks)

    def specs(order):
        return [
            pl.BlockSpec((1, CHUNK, hp), lambda b, s: (b, order(s), 0)),
            pl.BlockSpec((1, CHUNK, bc_w), lambda b, s: (b, order(s), b_blk)),
            pl.BlockSpec((1, CHUNK, bc_w), lambda b, s: (b, order(s), b_blk + 1)),
            pl.BlockSpec((1, CHUNK, LANES), lambda b, s: (b, order(s), 0)),
        ]

    return pl.pallas_call(
        _ssd_scan_kernel,
        out_shape=[jax.ShapeDtypeStruct((n_batch, s_len, hp), BF16)] * 2,
        grid=(n_batch, n_chunks),
        in_specs=specs(fwd) + specs(bwd) + [pl.BlockSpec((1, hp), lambda b, s: (0, 0))],
        out_specs=[pl.BlockSpec((1, CHUNK, hp), lambda b, s: (b, fwd(s), 0)),
                   pl.BlockSpec((1, CHUNK, hp), lambda b, s: (b, bwd(s), 0))],
        scratch_shapes=[pltpu.VMEM((SSD_STATE, hp), F32), pltpu.VMEM((SSD_STATE, hp), F32)],
        compiler_params=_cparams(("parallel", "arbitrary")),
    )(xbc, xbc, xbc, dtla, xbc, xbc, xbc, dtla, skip)


def _ret_chunk(q_ref, k_ref, v_ref, lg_ref, state_ref, y_ref, *, reverse):
    CHUNK = RET_CHUNK
    qk = q_ref.shape[2] // RET_HEADS
    vd = v_ref.shape[2] // RET_HEADS
    row = lax.broadcasted_iota(jnp.int32, (CHUNK, CHUNK), 0)
    col = lax.broadcasted_iota(jnp.int32, (CHUNK, CHUNK), 1)
    before = (col >= row) if reverse else (col <= row)
    dist = jnp.abs(row - col).astype(F32)
    n_in = ((CHUNK - row) if reverse else (row + 1)).astype(F32)
    n_end = ((row) if reverse else (CHUNK - 1 - row)).astype(F32)
    q = q_ref[0]
    k = k_ref[0]
    v = v_ref[0]
    d_sel = 1 if reverse else 0
    for h in range(RET_HEADS):
        lg = lg_ref[d_sel, h]
        qh = q[:, h * qk:(h + 1) * qk]
        kh = k[:, h * qk:(h + 1) * qk]
        vh = v[:, h * vd:(h + 1) * vd]
        scores = _dot_nt(qh, kh)
        m = (scores * jnp.exp(jnp.where(before, dist * lg, NEG_BIG))).astype(BF16)
        y_intra = _dot(m, vh)
        h_enter = state_ref[h]
        r_in = jnp.exp(n_in * lg)
        y_inter = _dot(qh, h_enter.astype(BF16))
        scale = jnp.concatenate([r_in] * (vd // CHUNK), axis=1)
        y_ref[0, :, h * vd:(h + 1) * vd] = (y_intra + y_inter * scale).astype(y_ref.dtype)
        r_end = jnp.exp(n_end * lg)
        k_scale = jnp.concatenate([r_end] * (qk // CHUNK), axis=1)
        kdec = (kh.astype(F32) * k_scale).astype(BF16)
        state_ref[h] = h_enter * jnp.exp(lg * CHUNK) + _dot_tn(kdec, vh)


def _ret_scan_kernel(lg_ref, q_f, k_f, v_f, q_b, k_b, v_b, yf_ref, yb_ref, st_f, st_b):
    @pl.when(pl.program_id(1) == 0)
    def _():
        st_f[...] = jnp.zeros_like(st_f)
        st_b[...] = jnp.zeros_like(st_b)

    _ret_chunk(q_f, k_f, v_f, lg_ref, st_f, yf_ref, reverse=False)
    _ret_chunk(q_b, k_b, v_b, lg_ref, st_b, yb_ref, reverse=True)


def _ret_scan(q, k, v, log_gamma, ctx_chunks):
    CHUNK = RET_CHUNK
    n_batch, s_len, qk_tot = q.shape
    v_tot = v.shape[2]
    n_chunks = s_len // CHUNK
    fwd = lambda s: s
    bwd = lambda s: _bwd_chunk(s, ctx_chunks, n_chunks)

    def specs(order):
        return [
            pl.BlockSpec((1, CHUNK, qk_tot), lambda b, s: (b, order(s), 0)),
            pl.BlockSpec((1, CHUNK, qk_tot), lambda b, s: (b, order(s), 0)),
            pl.BlockSpec((1, CHUNK, v_tot), lambda b, s: (b, order(s), 0)),
        ]

    return pl.pallas_call(
        _ret_scan_kernel,
        out_shape=[jax.ShapeDtypeStruct((n_batch, s_len, v_tot), BF16)] * 2,
        grid=(n_batch, n_chunks),
        in_specs=[pl.BlockSpec(memory_space=pltpu.SMEM)] + specs(fwd) + specs(bwd),
        out_specs=[pl.BlockSpec((1, CHUNK, v_tot), lambda b, s: (b, fwd(s), 0)),
                   pl.BlockSpec((1, CHUNK, v_tot), lambda b, s: (b, bwd(s), 0))],
        scratch_shapes=[pltpu.VMEM((RET_HEADS, qk_tot // RET_HEADS, v_tot // RET_HEADS), F32)] * 2,
        compiler_params=_cparams(("parallel", "arbitrary")),
    )(log_gamma, q, k, v, q, k, v)


def _readout_kernel(yf_ref, yb_ref, gate_ref, ng_ref, w_ref, x_ref, mod_ref, out_ref, *,
                    gate_first, group, gate_idx):
    y = yf_ref[0].astype(F32) + yb_ref[0].astype(F32)
    gate = _silu(gate_ref[0].astype(F32))
    if gate_first:
        y = y * gate
    pieces = []
    for j in range(y.shape[1] // group):
        yg = y[:, j * group:(j + 1) * group]
        ms = jnp.mean(yg * yg, axis=-1, keepdims=True)
        pieces.append(yg * lax.rsqrt(ms + NORM_EPS))
    y = jnp.concatenate(pieces, axis=1)
    y = y * ng_ref[...] if gate_first else y * gate
    o = _dot(y.astype(BF16), w_ref[...])
    out_ref[0] = x_ref[0] + mod_ref[0][gate_idx:gate_idx + 1] * o


def _readout(y_f, y_b, gate, norm_g, w_out, x, mod, ctx_tiles, gate_first, group):
    n_batch, s_len, inner = y_f.shape
    d = x.shape[2]
    n_tiles = s_len // ROW_TILE
    row = lambda b, t: (b, t, 0)
    return pl.pallas_call(
        functools.partial(_readout_kernel, gate_first=gate_first, group=group, gate_idx=2),
        out_shape=jax.ShapeDtypeStruct(x.shape, F32),
        grid=(n_batch, n_tiles),
        in_specs=[
            pl.BlockSpec((1, ROW_TILE, inner), row),
            pl.BlockSpec((1, ROW_TILE, inner), row),
            pl.BlockSpec((1, ROW_TILE, inner), row),
            pl.BlockSpec((1, inner), lambda b, t: (0, 0)),
            pl.BlockSpec((inner, d), lambda b, t: (0, 0)),
            pl.BlockSpec((1, ROW_TILE, d), row),
            pl.BlockSpec((1, 6, d), lambda b, t: (jnp.where(t < ctx_tiles, n_batch, b), 0, 0)),
        ],
        out_specs=pl.BlockSpec((1, ROW_TILE, d), row),
        compiler_params=_cparams(("parallel", "arbitrary")),
    )(y_f, y_b, gate, norm_g.reshape(1, inner), w_out, x, mod)


def _router_kernel(x_ref, g_ref, mod_ref, wr_ref, br_ref, h_ref, idx_ref, gate_ref, rank_ref,
                   count_ref, base_ref):
    @pl.when(jnp.logical_and(pl.program_id(0) == 0, pl.program_id(1) == 0))
    def _():
        base_ref[...] = jnp.zeros_like(base_ref)

    h = _norm_modulate(x_ref[0], g_ref[...], mod_ref[0], 3, 4)
    _store_token_tiles(h_ref, (), h)
    h_hi, h_lo = _split_bf16(h)
    w_hi, w_lo = _split_bf16(wr_ref[...])
    logits = _dot(h_hi, w_hi) + _dot(h_lo, w_hi) + _dot(h_hi, w_lo) + br_ref[...]
    lane = lax.broadcasted_iota(jnp.int32, logits.shape, 1).astype(F32)
    work = logits
    idx_out = jnp.zeros(logits.shape, F32)
    val_out = jnp.full(logits.shape, NEG_BIG, F32)
    picks = []
    for k in range(TOP_K):
        m = jnp.max(work, axis=-1, keepdims=True)
        first = jnp.min(jnp.where(work == m, lane, float(LANES)), axis=-1, keepdims=True)
        pick = lane == first
        picks.append(pick)
        idx_out = jnp.where(lane == k, first, idx_out)
        val_out = jnp.where(lane == k, m, val_out)
        work = jnp.where(pick, -jnp.inf, work)
    top = jnp.max(val_out, axis=-1, keepdims=True)
    e = jnp.exp(val_out - top)
    idx_ref[0] = idx_out.astype(jnp.int32)
    gate_ref[0] = e / jnp.sum(e, axis=-1, keepdims=True)

    rows = logits.shape[0]
    onehot = jnp.zeros(logits.shape, F32)
    for pick in picks:
        onehot = onehot + jnp.where(pick, 1.0, 0.0)
    r = lax.broadcasted_iota(jnp.int32, (rows, rows), 0)
    c = lax.broadcasted_iota(jnp.int32, (rows, rows), 1)
    strict = jnp.where(c < r, 1.0, 0.0).astype(BF16)
    earlier = _dot(strict, onehot.astype(BF16)) + base_ref[...]
    rank_out = jnp.zeros(logits.shape, F32)
    for k, pick in enumerate(picks):
        rk = jnp.sum(jnp.where(pick, earlier, 0.0), axis=-1, keepdims=True)
        rank_out = jnp.where(lane == k, rk, rank_out)
    rank_ref[0] = rank_out.astype(jnp.int32)
    base_ref[...] = base_ref[...] + jnp.sum(onehot, axis=0, keepdims=True)
    count_ref[...] = base_ref[...].astype(jnp.int32)


def _router(x, g, mod, w_router, b_router, ctx_tiles):
    n_batch, s_len, d = x.shape
    n_exp = w_router.shape[1]
    wr = jnp.zeros((d, LANES), F32).at[:, :n_exp].set(w_router)
    br = jnp.full((1, LANES), NEG_BIG, F32).at[0, :n_exp].set(b_router)
    row = lambda b, t: (b, t, 0)
    n_tiles = s_len // ROW_TILE
    return pl.pallas_call(
        _router_kernel,
        out_shape=[jax.ShapeDtypeStruct((n_batch * s_len * TOKEN_SUBLANES, LANES), F32),
                   jax.ShapeDtypeStruct((n_batch, s_len, LANES), jnp.int32),
                   jax.ShapeDtypeStruct((n_batch, s_len, LANES), F32),
                   jax.ShapeDtypeStruct((n_batch, s_len, LANES), jnp.int32),
                   jax.ShapeDtypeStruct((1, LANES), jnp.int32)],
        grid=(n_batch, s_len // ROW_TILE),
        in_specs=[
            pl.BlockSpec((1, ROW_TILE, d), row),
            pl.BlockSpec((1, d), lambda b, t: (0, 0)),
            pl.BlockSpec((1, 6, d), lambda b, t: (jnp.where(t < ctx_tiles, n_batch, b), 0, 0)),
            pl.BlockSpec((d, LANES), lambda b, t: (0, 0)),
            pl.BlockSpec((1, LANES), lambda b, t: (0, 0)),
        ],
        out_specs=[pl.BlockSpec((ROW_TILE * TOKEN_SUBLANES, LANES), lambda b, t: (b * n_tiles + t, 0)),
                   pl.BlockSpec((1, ROW_TILE, LANES), row),
                   pl.BlockSpec((1, ROW_TILE, LANES), row),
                   pl.BlockSpec((1, ROW_TILE, LANES), row),
                   pl.BlockSpec((1, LANES), lambda b, t: (0, 0))],
        scratch_shapes=[pltpu.VMEM((1, LANES), F32)],
        compiler_params=_cparams(("arbitrary", "arbitrary")),
    )(x, g.reshape(1, d), mod, wr, br)


def _on_slot(parity_of, fn):
    for s in range(2):
        @pl.when(parity_of % 2 == s)
        def _(s=s):
            fn(s)


def _indexed_row_pipeline(i, n, idx_hbm, idx_smem, isem, issue_rows):
    def idx_copy(tile, slot):
        return pltpu.make_async_copy(idx_hbm.at[tile], idx_smem[slot], isem.at[slot])

    @pl.when(i == 0)
    def _():
        idx_copy(0, 0).start()
        idx_copy(0, 0).wait()
        issue_rows(0, 0)

        @pl.when(n > 1)
        def _():
            idx_copy(1, 1).start()

    @pl.when(i + 1 < n)
    def _():
        def nxt(s):
            idx_copy(i + 1, s).wait()
            issue_rows(i + 1, s)
        _on_slot(i + 1, nxt)

    @pl.when(i + 2 < n)
    def _():
        _on_slot(i + 2, lambda s: idx_copy(i + 2, s).start())


def _dispatch_kernel(zblk_ref, nused_ref, pos_hbm, h_ref, xs_hbm, zero_buf, hbuf, idx_a, idx_b, zsem,
                     gsem, isem):
    i = pl.program_id(0)
    n = pl.num_programs(0)
    n_rows = TOP_K * ROW_TILE
    block_rows = EXPERT_ROWS * TOKEN_SUBLANES

    @pl.when(i == 0)
    def _():
        zero_buf[...] = jnp.zeros_like(zero_buf)

        def zero_block(blk):
            first = pl.multiple_of(blk * block_rows, block_rows)
            cp = pltpu.make_async_copy(zero_buf, xs_hbm.at[pl.ds(first, block_rows), :], zsem)
            cp.start()
            cp.wait()

        for e in range(zblk_ref.shape[0]):
            @pl.when(zblk_ref[e] >= 0)
            def _(e=e):
                zero_block(zblk_ref[e])

        def tail(blk, carry):
            zero_block(blk)
            return carry
        lax.fori_loop(nused_ref[0], xs_hbm.shape[0] // block_rows, tail, 0)

    def idx_copy(tile, slot):
        return pltpu.make_async_copy(pos_hbm.at[tile], (idx_a, idx_b)[slot], isem.at[slot])

    def wait_rows(slot):
        span = pl.ds(0, n_rows * TOKEN_SUBLANES)
        pltpu.make_async_copy(xs_hbm.at[span, :], xs_hbm.at[span, :], gsem.at[slot]).wait()

    @pl.when(i == 0)
    def _():
        idx_copy(0, 0).start()

    def step(slot):
        idx = (idx_a, idx_b)[slot]
        idx_copy(i, slot).wait()

        @pl.when(i >= 2)
        def _():
            wait_rows(slot)

        hbuf[slot] = h_ref[...]

        def body(it, carry):
            for j in range(DMA_UNROLL):
                r = it * DMA_UNROLL + j
                tok = it * (DMA_UNROLL // TOP_K) + j // TOP_K
                src = pl.multiple_of(tok * TOKEN_SUBLANES, TOKEN_SUBLANES)
                dst = pl.multiple_of(idx[r], TOKEN_SUBLANES)
                pltpu.make_async_copy(hbuf.at[slot, pl.ds(src, TOKEN_SUBLANES), :],
                                      xs_hbm.at[pl.ds(dst, TOKEN_SUBLANES), :],
                                      gsem.at[slot]).start(priority=j % 2)
            return carry
        lax.fori_loop(0, n_rows // DMA_UNROLL, body, 0)
    _on_slot(i, step)

    @pl.when(i + 1 < n)
    def _():
        _on_slot(i + 1, lambda s: idx_copy(i + 1, s).start())

    @pl.when(i == n - 1)
    def _():
        @pl.when(n >= 2)
        def _():
            _on_slot(i + 1, wait_rows)
        _on_slot(i, wait_rows)


def _dispatch(pos_tiles, zero_blocks, n_used, h_tiles, cap):
    n_tiles = pos_tiles.shape[0]
    tile_rows = ROW_TILE * TOKEN_SUBLANES
    grid_spec = pltpu.PrefetchScalarGridSpec(
        num_scalar_prefetch=2,
        grid=(n_tiles,),
        in_specs=[pl.BlockSpec(memory_space=pl.ANY),
                  pl.BlockSpec((tile_rows, LANES), lambda i, zb, nu: (i, 0))],
        out_specs=pl.BlockSpec(memory_space=pl.ANY),
        scratch_shapes=[
            pltpu.VMEM((EXPERT_ROWS * TOKEN_SUBLANES, LANES), F32),
            pltpu.VMEM((2, tile_rows, LANES), F32),
            pltpu.SMEM((TOP_K * ROW_TILE,), jnp.int32),
            pltpu.SMEM((TOP_K * ROW_TILE,), jnp.int32),
            pltpu.SemaphoreType.DMA,
            pltpu.SemaphoreType.DMA((2,)),
            pltpu.SemaphoreType.DMA((2,)),
        ],
    )
    return pl.pallas_call(
        _dispatch_kernel,
        out_shape=jax.ShapeDtypeStruct((cap * TOKEN_SUBLANES, LANES), F32),
        grid_spec=grid_spec,
        compiler_params=_cparams(("arbitrary",)),
    )(zero_blocks, n_used, pos_tiles, h_tiles)


def _expert_kernel(bexp_ref, nused_ref, x_ref, w1_ref, b1_ref, w2_ref, b2_ref, out_ref, w1p, w2p):
    i = pl.program_id(0)
    prev = bexp_ref[jnp.maximum(i - 1, 0)]
    f2 = w1_ref.shape[3]

    @pl.when(jnp.logical_or(i == 0, bexp_ref[i] != prev))
    def _():
        src = lax.broadcasted_iota(jnp.int32, (MXU_DIM, MXU_DIM), 0)
        dst = lax.broadcasted_iota(jnp.int32, (MXU_DIM, MXU_DIM), 1)
        want = jnp.where(dst < LANES, 2 * dst, 2 * (dst - LANES) + 1)
        perm = jnp.where(src == want, 1.0, 0.0).astype(BF16)
        for c in range(f2 // MXU_DIM):
            blk = w1_ref[0, 0, :, c * MXU_DIM:(c + 1) * MXU_DIM].astype(BF16)
            w1p[:, c * MXU_DIM:(c + 1) * MXU_DIM] = _dot(blk, perm).astype(BF16)
        w2p[...] = w2_ref[0, 0].astype(BF16)

    @pl.when(i < nused_ref[0])
    def _():
        x = jnp.concatenate(_load_token_tiles(x_ref, (), 0, EXPERT_ROWS), axis=1).astype(BF16)
        hdn = _dot(x, w1p[...]) + b1_ref[0]
        acts = []
        for c in range(f2 // MXU_DIM):
            glu = jnp.minimum(hdn[:, c * MXU_DIM:c * MXU_DIM + LANES], SWIGLU_LIMIT)
            lin = jnp.clip(hdn[:, c * MXU_DIM + LANES:(c + 1) * MXU_DIM], -SWIGLU_LIMIT, SWIGLU_LIMIT)
            acts.append(glu * _sigmoid(SWIGLU_ALPHA * glu) * (lin + 1.0))
        act = jnp.concatenate(acts, axis=1).astype(BF16)
        _store_token_tiles(out_ref, (), _dot(act, w2p[...]) + b2_ref[0])

    @pl.when(i >= nused_ref[0])
    def _():
        out_ref[...] = jnp.zeros_like(out_ref)


def _experts(x_sorted, block_expert, n_used, layer, w1_all, b1p, w2_all, b2):
    n_blocks = block_expert.shape[0]
    _, n_exp, d, f2 = w1_all.shape
    f = w2_all.shape[2]
    block_rows = EXPERT_ROWS * TOKEN_SUBLANES
    wspec = lambda shape: pl.BlockSpec(shape, lambda i, be, nu: (be[i], 0, 0))
    wspec_all = lambda shape: pl.BlockSpec(shape, lambda i, be, nu: (layer, be[i], 0, 0))
    grid_spec = pltpu.PrefetchScalarGridSpec(
        num_scalar_prefetch=2,
        grid=(n_blocks,),
        in_specs=[
            pl.BlockSpec((block_rows, LANES), lambda i, be, nu: (jnp.minimum(i, nu[0] - 1), 0)),
            wspec_all((1, 1, d, f2)), wspec((1, 1, f2)), wspec_all((1, 1, f, d)), wspec((1, 1, d)),
        ],
        out_specs=pl.BlockSpec((block_rows, LANES), lambda i, be, nu: (i, 0)),
        scratch_shapes=[pltpu.VMEM((d, f2), BF16), pltpu.VMEM((f, d), BF16)],
    )
    return pl.pallas_call(
        _expert_kernel,
        out_shape=jax.ShapeDtypeStruct((n_blocks * block_rows, LANES), F32),
        grid_spec=grid_spec,
        compiler_params=_cparams(("arbitrary",)),
    )(block_expert, n_used, x_sorted, w1_all, b1p, w2_all, b2.reshape(n_exp, 1, d))


def _combine_kernel(pos_hbm, y_hbm, gate_ref, x_ref, mod_ref, out_ref, ybuf, idx_a, idx_b, gsem, isem):
    i = pl.program_id(0)
    n = pl.num_programs(0)
    n_rows = TOP_K * ROW_TILE

    def issue_rows(tile, slot):
        idx = (idx_a, idx_b)[slot]

        def body(it, carry):
            for j in range(DMA_UNROLL):
                r = it * DMA_UNROLL + j
                src = pl.multiple_of(idx[r], TOKEN_SUBLANES)
                dst = pl.multiple_of(r * TOKEN_SUBLANES, TOKEN_SUBLANES)
                pltpu.make_async_copy(y_hbm.at[pl.ds(src, TOKEN_SUBLANES), :],
                                      ybuf.at[slot, pl.ds(dst, TOKEN_SUBLANES), :],
                                      gsem.at[slot]).start(priority=j % 2)
            return carry
        lax.fori_loop(0, n_rows // DMA_UNROLL, body, 0)

    def wait_rows(slot):
        pltpu.make_async_copy(y_hbm.at[pl.ds(0, n_rows * TOKEN_SUBLANES), :], ybuf.at[slot],
                              gsem.at[slot]).wait()

    _indexed_row_pipeline(i, n, pos_hbm, (idx_a, idx_b), isem, issue_rows)
    _on_slot(i, wait_rows)

    def compute(slot):
        g2 = mod_ref[0][5:6]
        sub = COMBINE_ROWS
        for rb in range(ROW_TILE // sub):
            rows = slice(rb * sub, (rb + 1) * sub)
            gate = gate_ref[rows, :]
            gk = [jnp.broadcast_to(gate[:, k:k + 1], (sub, LANES)) for k in range(TOP_K)]
            for s in range(TOKEN_SUBLANES):
                acc = None
                for k in range(TOP_K):
                    first = (k * ROW_TILE + rb * sub) * TOKEN_SUBLANES + s
                    part = ybuf[slot, pl.ds(first, sub, stride=TOKEN_SUBLANES), :] * gk[k]
                    acc = part if acc is None else acc + part
                cols = slice(s * LANES, (s + 1) * LANES)
                out_ref[rows, cols] = x_ref[rows, cols] + g2[:, cols] * acc
    _on_slot(i, compute)


def _combine(pos_tiles, y_rows, gate_flat, x_flat, mod, n_batch, tiles_per_batch, ctx_tiles):
    n_tiles = pos_tiles.shape[0]
    d = x_flat.shape[1]

    def mod_map(i):
        b = i // tiles_per_batch
        return (jnp.where(i % tiles_per_batch < ctx_tiles, n_batch, b), 0, 0)

    return pl.pallas_call(
        _combine_kernel,
        out_shape=jax.ShapeDtypeStruct(x_flat.shape, F32),
        grid=(n_tiles,),
        in_specs=[
            pl.BlockSpec(memory_space=pl.ANY),
            pl.BlockSpec(memory_space=pl.ANY),
            pl.BlockSpec((ROW_TILE, LANES), lambda i: (i, 0)),
            pl.BlockSpec((ROW_TILE, d), lambda i: (i, 0)),
            pl.BlockSpec((1, 6, d), mod_map),
        ],
        out_specs=pl.BlockSpec((ROW_TILE, d), lambda i: (i, 0)),
        scratch_shapes=[
            pltpu.VMEM((2, TOP_K * ROW_TILE * TOKEN_SUBLANES, LANES), F32),
            pltpu.SMEM((TOP_K * ROW_TILE,), jnp.int32),
            pltpu.SMEM((TOP_K * ROW_TILE,), jnp.int32),
            pltpu.SemaphoreType.DMA((2,)),
            pltpu.SemaphoreType.DMA((2,)),
        ],
        compiler_params=_cparams(("arbitrary",)),
    )(pos_tiles, y_rows, gate_flat, x_flat, mod)


def _moe(x, g, mod, w_router, b_router, layer, w1_all, b1p, w2_all, b2, ctx_tiles):
    n_batch, s_len, d = x.shape
    n_tok = n_batch * s_len
    n_exp = w1_all.shape[1]
    n_tiles = n_tok // ROW_TILE
    h, idx, gate, rank, count = _router(x, g, mod, w_router, b_router, ctx_tiles)

    counts = count[0, :n_exp]
    padded = (counts + EXPERT_ROWS - 1) // EXPERT_ROWS * EXPERT_ROWS
    pend = jnp.cumsum(padded)
    pstart = pend - padded
    n_blocks = -(-(n_tok * TOP_K) // EXPERT_ROWS) + n_exp
    block_start = jnp.arange(n_blocks, dtype=jnp.int32) * EXPERT_ROWS
    block_expert = jnp.minimum(jnp.sum(block_start[:, None] >= pend[None, :], axis=1),
                               n_exp - 1).astype(jnp.int32)
    n_used = (pend[-1] // EXPERT_ROWS).astype(jnp.int32).reshape(1)
    zero_blocks = jnp.where(padded > counts, pend // EXPERT_ROWS - 1, -1).astype(jnp.int32)

    top_i = idx[..., :TOP_K].reshape(n_tok, TOP_K)
    pos = ((pstart[top_i] + rank[..., :TOP_K].reshape(n_tok, TOP_K)) * TOKEN_SUBLANES).astype(jnp.int32)
    pos_token_major = pos.reshape(n_tiles, ROW_TILE * TOP_K)
    pos_slot_major = pos.reshape(n_tiles, ROW_TILE, TOP_K).transpose(0, 2, 1).reshape(n_tiles, -1)

    x_sorted = _dispatch(pos_token_major, zero_blocks, n_used, h, n_blocks * EXPERT_ROWS)
    y_rows = _experts(x_sorted, block_expert, n_used, layer, w1_all, b1p, w2_all, b2)
    out = _combine(pos_slot_major, y_rows, gate.reshape(n_tok, LANES), x.reshape(n_tok, d), mod,
                   n_batch, s_len // ROW_TILE, ctx_tiles)
    return out.reshape(n_batch, s_len, d)


def _final_norm_kernel(x_ref, g_ref, out_ref):
    x = x_ref[0]
    ms = jnp.mean(x * x, axis=-1, keepdims=True)
    out_ref[0] = x * lax.rsqrt(ms + NORM_EPS) * g_ref[...]


def _final_norm(x, g, ctx_tiles, seq_len):
    n_batch, _, d = x.shape
    return pl.pallas_call(
        _final_norm_kernel,
        out_shape=jax.ShapeDtypeStruct((n_batch, seq_len, d), F32),
        grid=(n_batch, seq_len // ROW_TILE),
        in_specs=[pl.BlockSpec((1, ROW_TILE, d), lambda b, t: (b, t + ctx_tiles, 0)),
                  pl.BlockSpec((1, d), lambda b, t: (0, 0))],
        out_specs=pl.BlockSpec((1, ROW_TILE, d), lambda b, t: (b, t, 0)),
        compiler_params=_cparams(("parallel", "arbitrary")),
    )(x, g.reshape(1, d))


def _rope_tables(seq_len, ctx_len, half_dim):
    rows = seq_len // GRID_W
    t = jnp.arange(seq_len)
    row_pos = (t // GRID_W).astype(F32) - (rows - 1) / 2.0
    col_pos = (t % GRID_W).astype(F32) - (GRID_W - 1) / 2.0
    inv_freq = ROPE_BASE ** (-jnp.arange(0, half_dim, 2, dtype=F32) / half_dim)

    def tables(pos):
        ang = pos[:, None] * inv_freq[None, :]
        c, s = jnp.cos(ang), jnp.sin(ang)
        return jnp.concatenate([c, c], axis=1), jnp.concatenate([-s, s], axis=1)

    cr, sr = tables(row_pos)
    cc, sc = tables(col_pos)
    cos = jnp.concatenate([cr, cc], axis=1)
    sin = jnp.concatenate([sr, sc], axis=1)
    cos = jnp.concatenate([jnp.ones((ctx_len, cos.shape[1]), F32), cos], axis=0)
    sin = jnp.concatenate([jnp.zeros((ctx_len, sin.shape[1]), F32), sin], axis=0)
    return cos, sin


def kernel(x, c, ctx, c_ctx, w_mod, b_mod, norm1_g, norm2_g, ssd_w_in, ssd_conv_w, ssd_conv_b,
           ssd_dt_bias_f, ssd_dt_bias_b, ssd_a_log_f, ssd_a_log_b, ssd_d, ssd_norm_g, ssd_w_out,
           ret_w_in, ret_scale_f, ret_scale_b, ret_w_out, router_w, router_b,
           moe_w1, moe_b1, moe_w2, moe_b2, final_g):
    n_batch, seq_len, d = x.shape
    ctx_len = ctx.shape[1]
    depth = w_mod.shape[0]
    assert ctx_len % ROW_TILE == 0 and seq_len % ROW_TILE == 0
    ctx_tiles = ctx_len // ROW_TILE
    ctx_chunks = ctx_len // CHUNK

    n_heads = ssd_d.shape[1]
    ssd_inner = n_heads * SSD_HEAD_DIM
    bc_w = SSD_GROUPS * SSD_STATE
    conv_dim = ssd_inner + 2 * bc_w
    assert n_heads == LANES // 4 and ssd_inner % bc_w == 0

    ret_qk = (ret_w_in.shape[2] - 2 * ret_w_out.shape[1]) // 2
    ret_v = ret_w_out.shape[1]
    qk_dim = ret_qk // RET_HEADS
    assert qk_dim == 2 * LANES

    n_exp, _, f2 = moe_w1.shape[1:]
    assert f2 % MXU_DIM == 0 and d == TOKEN_SUBLANES * LANES

    cond = jnp.concatenate([c, c_ctx[None, :], jnp.zeros((16 - n_batch - 1, d), F32)], axis=0)
    mod_all = _modulation(cond, w_mod, b_mod).reshape(depth, 16, 6, d)

    xs = jnp.concatenate([ctx, x], axis=1)
    cos, sin = _rope_tables(seq_len, ctx_len, qk_dim // 2)

    for i in range(depth):
        j = i // 2
        mod = mod_all[i]
        if i % 2 == 0:
            w_in = ssd_w_in[j]
            w_dt = w_in[:, ssd_inner + conv_dim:].astype(BF16)
            weights = [w_in[:, :ssd_inner].astype(BF16),
                       w_in[:, ssd_inner:ssd_inner + conv_dim].astype(BF16),
                       jnp.concatenate([w_dt, w_dt], axis=1)]
            z, xbc, dt_raw = _in_proj(xs, norm1_g[i], mod, weights, [BF16, BF16, F32], ctx_tiles)
            dt_bias2 = jnp.concatenate([ssd_dt_bias_f[j], ssd_dt_bias_b[j]])
            dt_bias4 = jnp.concatenate([dt_bias2, dt_bias2]).reshape(1, LANES)
            a_log4 = jnp.concatenate([jnp.zeros((2 * n_heads,), F32), ssd_a_log_f[j],
                                      ssd_a_log_b[j]]).reshape(1, LANES)
            xbc_c, dtla = _ssd_conv(xbc, dt_raw, ssd_conv_w[j], ssd_conv_b[j], dt_bias4, a_log4,
                                    ctx_chunks)
            skip = jnp.repeat(ssd_d[j], SSD_HEAD_DIM).reshape(1, ssd_inner)
            y_f, y_b = _ssd_scan(xbc_c, dtla, skip, ssd_inner, ctx_chunks)
            xs = _readout(y_f, y_b, z, ssd_norm_g[j], ssd_w_out[j].astype(BF16), xs, mod,
                          ctx_tiles, True, ssd_inner // SSD_GROUPS)
        else:
            w_in = ret_w_in[j]
            bounds = [0, ret_qk, 2 * ret_qk, 2 * ret_qk + ret_v, 2 * ret_qk + 2 * ret_v]
            weights = [w_in[:, lo:hi].astype(BF16) for lo, hi in zip(bounds[:-1], bounds[1:])]
            q, k, v, gt = _in_proj(xs, norm1_g[i], mod, weights, [BF16] * 4, ctx_tiles,
                                   rope=(cos, sin, (0, 1)), k_scale=qk_dim ** -0.5)
            log_gamma = jnp.stack([jnp.log1p(-jnp.exp2(-ret_scale_f[j].astype(F32))),
                                   jnp.log1p(-jnp.exp2(-ret_scale_b[j].astype(F32)))])
            y_f, y_b = _ret_scan(q, k, v, log_gamma, ctx_len // RET_CHUNK)
            xs = _readout(y_f, y_b, gt, jnp.ones((ret_v,), F32), ret_w_out[j].astype(BF16), xs, mod,
                          ctx_tiles, False, ret_v // RET_HEADS)
        b1p = moe_b1[i].reshape(n_exp, f2 // MXU_DIM, LANES, 2).transpose(0, 1, 3, 2).reshape(n_exp, 1, f2)
        xs = _moe(xs, norm2_g[i], mod, router_w[i], router_b[i], i, moe_w1, b1p, moe_w2, moe_b2[i],
                  ctx_tiles)
    return _final_norm(xs, final_g, ctx_tiles, seq_len)
```
